```python
import math
import jax
import jax.numpy as jnp
from jax import lax
import numpy as np

D_MODEL = 2048
BATCH = 16
SEQ = 2048
DEPTH = 1

MLA_HEADS = 8
MLA_Q_RANK = 512
MLA_KV_RANK = 256
MLA_NOPE_DIM = 128
MLA_ROPE_DIM = 64
MLA_V_DIM = 128
ROPE_THETA = 10000.0
SWA_Q_HEADS = 16
SWA_KV_HEADS = 4
SWA_HEAD_DIM = 64
WINDOW = 128
BLOCK = 128
REL_BUCKETS = 32
REL_MAX_DIST = 128
N_BRANCHES = 2
N_EXPERTS = 64
TOP_K = 8
N_GROUPS = 8
TOPK_GROUPS = 4
EXPERT_FF = 512
SHARED_FF = 512
ROUTED_SCALE = 2.5
EXPERT_BLOCK = 256
ALPHA = (2 * DEPTH) ** 0.25
BETA = (8 * DEPTH) ** -0.25
LN_EPS = 1e-5
RMS_EPS = 1e-6

COL_SIZES = (MLA_Q_RANK, MLA_KV_RANK, MLA_ROPE_DIM, SWA_Q_HEADS * SWA_HEAD_DIM,
             SWA_KV_HEADS * SWA_HEAD_DIM, SWA_KV_HEADS * SWA_HEAD_DIM, N_BRANCHES * D_MODEL)
IN_COLS = sum(COL_SIZES)

kernel_name = "hybrid_mla_swa_gated_moe_deepnorm"


def layer_norm(x, g, b):
    xf = x.astype(jnp.float32)
    mu = jnp.mean(xf, -1, keepdims=True)
    var = jnp.mean(jnp.square(xf - mu), -1, keepdims=True)
    return ((xf - mu) * lax.rsqrt(var + LN_EPS) * g.astype(jnp.float32) + b.astype(jnp.float32)).astype(x.dtype)


def rms_norm(x, g):
    xf = x.astype(jnp.float32)
    return (xf * lax.rsqrt(jnp.mean(jnp.square(xf), -1, keepdims=True) + RMS_EPS) * g.astype(jnp.float32)).astype(x.dtype)


def rope(x, positions):
    d = x.shape[-1]
    inv = ROPE_THETA ** (-jnp.arange(0, d, 2, dtype=jnp.float32) / d)
    ang = positions.astype(jnp.float32)[..., None] * inv
    cos = jnp.cos(ang)[:, :, None, :]
    sin = jnp.sin(ang)[:, :, None, :]
    x1, x2 = jnp.split(x.astype(jnp.float32), 2, axis=-1)
    return jnp.concatenate([x1 * cos - x2 * sin, x1 * sin + x2 * cos], -1).astype(x.dtype)


def t5_bucket(dist):
    n = jnp.maximum(dist, 0)
    max_exact = REL_BUCKETS // 2
    large = max_exact + (jnp.log(jnp.maximum(n, 1).astype(jnp.float32) / max_exact)
                         / math.log(REL_MAX_DIST / max_exact) * (REL_BUCKETS - max_exact)).astype(jnp.int32)
    large = jnp.minimum(large, REL_BUCKETS - 1)
    return jnp.where(n < max_exact, n, large)


def causal_block_attention(q, k, v, scale):
    B, S, H, dk = q.shape
    nb = S // BLOCK
    qb = q.reshape(B, nb, BLOCK, H, dk).transpose(1, 0, 2, 3, 4)
    kpos = jnp.arange(S)

    def one_block(args):
        qi, i = args
        s = jnp.einsum('bqhd,bkhd->bhqk', qi, k, preferred_element_type=jnp.float32) * scale
        qpos = i * BLOCK + jnp.arange(BLOCK)
        s = jnp.where(kpos[None, :] <= qpos[:, None], s, -jnp.inf)
        p = jax.nn.softmax(s, axis=-1).astype(v.dtype)
        return jnp.einsum('bhqk,bkhd->bqhd', p, v)

    out = lax.map(one_block, (qb, jnp.arange(nb)))
    return out.transpose(1, 0, 2, 3, 4).reshape(B, S, H, v.shape[-1])


def swa_sink_attention(q, k, v, sinks, rel_table):
    B, S, HQ, dh = q.shape
    G = HQ // SWA_KV_HEADS
    nb = S // BLOCK
    qb = q.reshape(B, nb, BLOCK, SWA_KV_HEADS, G, dh)

    def band(t):
        tb = t.reshape(B, nb, BLOCK, SWA_KV_HEADS, dh)
        prev = jnp.pad(tb[:, :-1], ((0, 0), (1, 0), (0, 0), (0, 0), (0, 0)))
        return jnp.concatenate([prev, tb], axis=2)

    kb, vb = band(k), band(v)
    s = jnp.einsum('bnqhgd,bnkhd->bnhgqk', qb, kb, preferred_element_type=jnp.float32) * (dh ** -0.5)
    qi = jnp.arange(BLOCK)[:, None]
    kj = jnp.arange(2 * BLOCK)[None, :]
    dist = qi + BLOCK - kj
    bias = rel_table[t5_bucket(dist)].astype(jnp.float32)
    bias = bias.transpose(2, 0, 1).reshape(SWA_KV_HEADS, G, BLOCK, 2 * BLOCK)
    in_window = (dist >= 0) & (dist < WINDOW)
    has_prev = (jnp.arange(nb)[:, None, None] > 0) | (kj[None] >= BLOCK)
    mask = in_window[None] & has_prev
    s = jnp.where(mask[None, :, None, None], s + bias, -jnp.inf)
    sink = jnp.broadcast_to(sinks.astype(jnp.float32).reshape(SWA_KV_HEADS, G, 1, 1), s.shape[:-1] + (1,))
    p = jax.nn.softmax(jnp.concatenate([s, sink], axis=-1), axis=-1)[..., :-1].astype(v.dtype)
    o = jnp.einsum('bnhgqk,bnkhd->bnqhgd', p, vb)
    return o.reshape(B, S, HQ, dh)


def token_mixer(x, positions, w_in, b_gate, q_norm_g, kv_norm_g, w_uq, w_uk, w_uv,
                swa_sinks, rel_table, w_br_mla, w_br_swa, w_out):
    B, S, _ = x.shape
    proj = x @ w_in
    bounds = np.cumsum(COL_SIZES)[:-1].tolist()
    c_q, c_kv, k_r, q_s, k_s, v_s, gate = jnp.split(proj, bounds, axis=-1)

    q = (rms_norm(c_q, q_norm_g) @ w_uq).reshape(B, S, MLA_HEADS, MLA_NOPE_DIM + MLA_ROPE_DIM)
    q_nope, q_rope = jnp.split(q, [MLA_NOPE_DIM], axis=-1)
    c_kv = rms_norm(c_kv, kv_norm_g)
    k_nope = (c_kv @ w_uk).reshape(B, S, MLA_HEADS, MLA_NOPE_DIM)
    v_m = (c_kv @ w_uv).reshape(B, S, MLA_HEADS, MLA_V_DIM)
    k_rope = rope(k_r[:, :, None, :], positions)
    q_m = jnp.concatenate([q_nope, rope(q_rope, positions)], axis=-1)
    k_m = jnp.concatenate([k_nope, jnp.broadcast_to(k_rope, (B, S, MLA_HEADS, MLA_ROPE_DIM))], axis=-1)
    o_m = causal_block_attention(q_m, k_m, v_m, (MLA_NOPE_DIM + MLA_ROPE_DIM) ** -0.5)

    o_s = swa_sink_attention(q_s.reshape(B, S, SWA_Q_HEADS, SWA_HEAD_DIM),
                             k_s.reshape(B, S, SWA_KV_HEADS, SWA_HEAD_DIM),
                             v_s.reshape(B, S, SWA_KV_HEADS, SWA_HEAD_DIM),
                             swa_sinks, rel_table)

    y_m = o_m.reshape(B, S, -1) @ w_br_mla
    y_s = o_s.reshape(B, S, -1) @ w_br_swa
    g = jax.nn.sigmoid(gate.reshape(B, S, N_BRANCHES, D_MODEL) + b_gate)
    merged = g[:, :, 0] * y_m + g[:, :, 1] * y_s
    return merged @ w_out


def moe_ffn(x, w_router, router_bias, w_gate_up, w_down, w_shared_gate_up, w_shared_down):
    B, S, D = x.shape
    xt = x.reshape(-1, D)
    T = xt.shape[0]
    scores = jax.nn.sigmoid((xt @ w_router).astype(jnp.float32))
    sel = scores + router_bias.astype(jnp.float32)
    grp_score = lax.top_k(sel.reshape(T, N_GROUPS, -1), 2)[0].sum(-1)
    _, top_g = lax.top_k(grp_score, TOPK_GROUPS)
    gmask = jax.nn.one_hot(top_g, N_GROUPS).sum(-2) > 0
    emask = jnp.repeat(gmask, N_EXPERTS // N_GROUPS, axis=-1)
    _, top_e = lax.top_k(jnp.where(emask, sel, -jnp.inf), TOP_K)
    w = jnp.take_along_axis(scores, top_e, axis=-1)
    w = w / jnp.sum(w, -1, keepdims=True) * ROUTED_SCALE

    N = T * TOP_K
    NB = -(-N // EXPERT_BLOCK) + N_EXPERTS
    NP = NB * EXPERT_BLOCK
    flat_e = top_e.reshape(-1)
    order = jnp.argsort(flat_e)
    sorted_e = flat_e[order]
    gs = jnp.bincount(flat_e, length=N_EXPERTS)
    ps = (gs + EXPERT_BLOCK - 1) // EXPERT_BLOCK * EXPERT_BLOCK
    start = jnp.cumsum(gs) - gs
    pend = jnp.cumsum(ps)
    pstart = pend - ps
    dest = pstart[sorted_e] + jnp.arange(N) - start[sorted_e]
    row_tok = jnp.full((NP,), T, jnp.int32).at[dest].set((order // TOP_K).astype(jnp.int32))
    row_w = jnp.zeros((NP,), jnp.float32).at[dest].set(w.reshape(-1)[order])
    block_e = jnp.minimum(jnp.searchsorted(pend, jnp.arange(NB) * EXPERT_BLOCK, side='right'), N_EXPERTS - 1)
    x_pad = jnp.concatenate([xt, jnp.zeros((1, D), xt.dtype)], axis=0)

    def expert_block(args):
        toks, wts, e = args
        h = x_pad[toks] @ w_gate_up[e]
        hg, hu = jnp.split(h, 2, axis=-1)
        return ((jax.nn.silu(hg) * hu) @ w_down[e]) * wts[:, None].astype(xt.dtype)

    ys = lax.map(expert_block, (row_tok.reshape(NB, EXPERT_BLOCK), row_w.reshape(NB, EXPERT_BLOCK), block_e))
    routed = jax.ops.segment_sum(ys.reshape(NP, D), row_tok, num_segments=T + 1)[:T]

    sg, su = jnp.split(xt @ w_shared_gate_up, 2, axis=-1)
    shared = (jax.nn.silu(sg) * su) @ w_shared_down
    return (routed + shared).reshape(B, S, D)


def setup_inputs(seed: int = 0) -> dict:
    key = jax.random.key(seed)
    ks = iter(jax.random.split(key, 32))
    f32 = jnp.float32
    L, D = DEPTH, D_MODEL

    def nrm(shape, scale):
        return jax.random.normal(next(ks), shape, f32) * scale

    x = nrm((BATCH, SEQ, D), 1.0)
    positions = jnp.broadcast_to(jnp.arange(SEQ, dtype=jnp.int32), (BATCH, SEQ))
    col_scale = jnp.concatenate([jnp.full((c,), BETA if i == 5 else 1.0, f32) for i, c in enumerate(COL_SIZES)])
    w_in = nrm((L, D, IN_COLS), D ** -0.5) * col_scale
    b_gate = nrm((L, N_BRANCHES, D), 0.1)
    q_norm_g = 1.0 + nrm((L, MLA_Q_RANK), 0.01)
    kv_norm_g = 1.0 + nrm((L, MLA_KV_RANK), 0.01)
    w_uq = nrm((L, MLA_Q_RANK, MLA_HEADS * (MLA_NOPE_DIM + MLA_ROPE_DIM)), MLA_Q_RANK ** -0.5)
    w_uk = nrm((L, MLA_KV_RANK, MLA_HEADS * MLA_NOPE_DIM), MLA_KV_RANK ** -0.5)
    w_uv = nrm((L, MLA_KV_RANK, MLA_HEADS * MLA_V_DIM), MLA_KV_RANK ** -0.5 * BETA)
    swa_sinks = nrm((L, SWA_Q_HEADS), 0.5)
    rel_table = nrm((REL_BUCKETS, SWA_Q_HEADS), 0.5)
    w_br_mla = nrm((L, MLA_HEADS * MLA_V_DIM, D), (MLA_HEADS * MLA_V_DIM) ** -0.5)
    w_br_swa = nrm((L, SWA_Q_HEADS * SWA_HEAD_DIM, D), (SWA_Q_HEADS * SWA_HEAD_DIM) ** -0.5)
    w_out = nrm((L, D, D), D ** -0.5 * BETA)
    ln1_g = 1.0 + nrm((L, D), 0.01)
    ln1_b = nrm((L, D), 0.01)
    w_router = nrm((L, D, N_EXPERTS), D ** -0.5)
    router_bias = nrm((L, N_EXPERTS), 0.01)
    w_gate_up = nrm((L, N_EXPERTS, D, 2 * EXPERT_FF), D ** -0.5)
    w_down = nrm((L, N_EXPERTS, EXPERT_FF, D), EXPERT_FF ** -0.5 * BETA)
    w_shared_gate_up = nrm((L, D, 2 * SHARED_FF), D ** -0.5)
    w_shared_down = nrm((L, SHARED_FF, D), SHARED_FF ** -0.5 * BETA)
    ln2_g = 1.0 + nrm((L, D), 0.01)
    ln2_b = nrm((L, D), 0.01)
    return {"x": x, "positions": positions, "w_in": w_in, "b_gate": b_gate,
            "q_norm_g": q_norm_g, "kv_norm_g": kv_norm_g, "w_uq": w_uq, "w_uk": w_uk,
            "w_uv": w_uv, "swa_sinks": swa_sinks, "rel_table": rel_table,
            "w_br_mla": w_br_mla, "w_br_swa": w_br_swa, "w_out": w_out,
            "ln1_g": ln1_g, "ln1_b": ln1_b, "w_router": w_router, "router_bias": router_bias,
            "w_gate_up": w_gate_up, "w_down": w_down, "w_shared_gate_up": w_shared_gate_up,
            "w_shared_down": w_shared_down, "ln2_g": ln2_g, "ln2_b": ln2_b}


def reference(x, positions, w_in, b_gate, q_norm_g, kv_norm_g, w_uq, w_uk, w_uv, swa_sinks,
              rel_table, w_br_mla, w_br_swa, w_out, ln1_g, ln1_b, w_router, router_bias,
              w_gate_up, w_down, w_shared_gate_up, w_shared_down, ln2_g, ln2_b):
    h = x
    for l in range(DEPTH):
        mix = token_mixer(h, positions, w_in[l], b_gate[l], q_norm_g[l], kv_norm_g[l], w_uq[l],
                          w_uk[l], w_uv[l], swa_sinks[l], rel_table, w_br_mla[l], w_br_swa[l], w_out[l])
        h = layer_norm(ALPHA * h + mix, ln1_g[l], ln1_b[l])
        ffn = moe_ffn(h, w_router[l], router_bias[l], w_gate_up[l], w_down[l],
                      w_shared_gate_up[l], w_shared_down[l])
        h = layer_norm(ALPHA * h + ffn, ln2_g[l], ln2_b[l])
    return h
```

```python
import functools
import math

import jax
import jax.numpy as jnp
import numpy as np
from jax import lax
from jax.experimental import pallas as pl
from jax.experimental.pallas import tpu as pltpu

F32 = jnp.float32
BF16 = jnp.bfloat16

MLA_HEADS = 8
MLA_NOPE_DIM = 128
MLA_ROPE_DIM = 64
MLA_V_DIM = 128
ROPE_THETA = 10000.0
SWA_Q_HEADS = 16
SWA_KV_HEADS = 4
SWA_HEAD_DIM = 64
WINDOW = 128
BLOCK = 128
REL_BUCKETS = 32
REL_MAX_DIST = 128
N_BRANCHES = 2
N_EXPERTS = 64
TOP_K = 8
N_GROUPS = 8
TOPK_GROUPS = 4
ROUTED_SCALE = 2.5
DEPTH = 1
ALPHA = (2 * DEPTH) ** 0.25
LN_EPS = 1e-5
RMS_EPS = 1e-6

LANE = 128
MXU_DIM = 256
NEG_BIG = -1e30

PROJ_TM = 512
MLA_PROJ_TM = 512
MLA_TQ = 512
MERGE_TM = 512
OUT_TM = 512
MOE_TM = 256
COMBINE_TM = 256
VMEM_LIMIT = 56 * 1024 * 1024


def _cparams(sem):
    return pltpu.CompilerParams(dimension_semantics=sem, vmem_limit_bytes=VMEM_LIMIT)


def _tile(n, t):
    t = min(n, t)
    assert n % t == 0, (n, t)
    return t


def _in_proj_kernel(x_ref, w_ref, o_ref):
    x = x_ref[...].astype(BF16)
    o_ref[...] = jnp.dot(x, w_ref[...], preferred_element_type=F32).astype(o_ref.dtype)


def _in_proj(x2, w_p):
    T, D = x2.shape
    N = w_p.shape[1]
    tn = N // 2 if (N // 2) % LANE == 0 and N % 2 == 0 else N
    tm = _tile(T, PROJ_TM)
    return pl.pallas_call(
        _in_proj_kernel,
        out_shape=jax.ShapeDtypeStruct((T, N), BF16),
        grid=(N // tn, T // tm),
        in_specs=[pl.BlockSpec((tm, D), lambda j, i: (i, 0)),
                  pl.BlockSpec((D, tn), lambda j, i: (0, j))],
        out_specs=pl.BlockSpec((tm, tn), lambda j, i: (i, j)),
        compiler_params=_cparams(("arbitrary", "arbitrary")),
        name="in_proj",
    )(x2, w_p)


def _mla_proj_kernel(cq_ref, ckv_ref, kra_ref, krb_ref, cos_ref, sin_ref, qg_ref, kvg_ref,
                     wqa_ref, wqb_ref, wuk_ref, wuv_ref, q_ref, k_ref, v_ref, *, heads, scale):
    cos = cos_ref[...]
    sin = sin_ref[...]
    cq = cq_ref[...].astype(F32)
    qn = cq * lax.rsqrt(jnp.mean(cq * cq, axis=-1, keepdims=True) + RMS_EPS) * qg_ref[...]
    qn = qn.astype(BF16)
    qa = jnp.dot(qn, wqa_ref[...], preferred_element_type=F32)
    qb = jnp.dot(qn, wqb_ref[...], preferred_element_type=F32)
    ckv = ckv_ref[...].astype(F32)
    cn = ckv * lax.rsqrt(jnp.mean(ckv * ckv, axis=-1, keepdims=True) + RMS_EPS) * kvg_ref[...]
    cn = cn.astype(BF16)
    kn = jnp.dot(cn, wuk_ref[...], preferred_element_type=F32)
    v_ref[...] = jnp.dot(cn, wuv_ref[...], preferred_element_type=F32).astype(v_ref.dtype)
    krope = (kra_ref[...].astype(F32) * cos + krb_ref[...].astype(F32) * sin).astype(k_ref.dtype)
    for h in range(heads):
        lo = h * 2 * LANE
        q_ref[:, lo:lo + LANE] = (qa[:, lo:lo + LANE] * scale).astype(q_ref.dtype)
        q_ref[:, lo + LANE:lo + 2 * LANE] = (
            (qa[:, lo + LANE:lo + 2 * LANE] * cos + qb[:, h * LANE:(h + 1) * LANE] * sin) * scale
        ).astype(q_ref.dtype)
        k_ref[:, lo:lo + LANE] = kn[:, h * LANE:(h + 1) * LANE].astype(k_ref.dtype)
        k_ref[:, lo + LANE:lo + 2 * LANE] = krope


def _mla_proj(proj, lay, cosp, sinp, qg, kvg, wqa, wqb, wuk, wuv, heads):
    T = proj.shape[0]
    tm = _tile(T, MLA_PROJ_TM)
    qr, kvr = wqa.shape[0], wuk.shape[0]

    def col(name, width):
        off = lay[name]
        assert off % width == 0
        return pl.BlockSpec((tm, width), lambda i, o=off // width: (i, o))

    row = lambda w: pl.BlockSpec((tm, w), lambda i: (i, 0))
    full = lambda a: pl.BlockSpec(a.shape, lambda i: (0,) * a.ndim)
    scale = (MLA_NOPE_DIM + MLA_ROPE_DIM) ** -0.5
    return pl.pallas_call(
        functools.partial(_mla_proj_kernel, heads=heads, scale=scale),
        out_shape=(jax.ShapeDtypeStruct((T, heads * 2 * LANE), BF16),
                   jax.ShapeDtypeStruct((T, heads * 2 * LANE), BF16),
                   jax.ShapeDtypeStruct((T, heads * MLA_V_DIM), BF16)),
        grid=(T // tm,),
        in_specs=[col("c_q", qr), col("c_kv", kvr), col("kra", LANE), col("krb", LANE),
                  row(LANE), row(LANE), full(qg), full(kvg), full(wqa), full(wqb), full(wuk), full(wuv)],
        out_specs=(row(heads * 2 * LANE), row(heads * 2 * LANE), row(heads * MLA_V_DIM)),
        compiler_params=_cparams(("arbitrary",)),
        name="mla_proj",
    )(proj, proj, proj, proj, cosp, sinp, qg, kvg, wqa, wqb, wuk, wuv)


def _mla_attn_kernel(q_ref, k_ref, v_ref, o_ref, *, tq):
    i = pl.program_id(2)
    q = q_ref[0]
    dn = (((1,), (1,)), ((), ()))

    def step(kb, vb, carry, mask):
        m, l, acc = carry
        s = lax.dot_general(q, kb, dn, preferred_element_type=F32)
        if mask is not None:
            s = jnp.where(mask, s, NEG_BIG)
        m_new = jnp.maximum(m, jnp.max(s, axis=-1, keepdims=True))
        p = jnp.exp(s - m_new)
        a = jnp.exp(m - m_new)
        l = a * l + jnp.sum(p, axis=-1, keepdims=True)
        acc = a * acc + jnp.dot(p.astype(vb.dtype), vb, preferred_element_type=F32)
        return m_new, l, acc

    def body(j, carry):
        off = pl.multiple_of(j * tq, tq)
        return step(k_ref[0, pl.ds(off, tq), :], v_ref[0, pl.ds(off, tq), :], carry, None)

    init = (jnp.full((tq, 1), NEG_BIG, F32), jnp.zeros((tq, 1), F32),
            jnp.zeros((tq, v_ref.shape[-1]), F32))
    carry = lax.fori_loop(0, i, body, init)
    off = pl.multiple_of(i * tq, tq)
    r = lax.broadcasted_iota(jnp.int32, (tq, tq), 0)
    c = lax.broadcasted_iota(jnp.int32, (tq, tq), 1)
    m, l, acc = step(k_ref[0, pl.ds(off, tq), :], v_ref[0, pl.ds(off, tq), :], carry, c <= r)
    o_ref[0] = (acc / l).astype(o_ref.dtype)


def _mla_attn(q, k, v, heads):
    B, S, _ = q.shape
    tq = _tile(S, MLA_TQ)
    dv = v.shape[-1] // heads
    return pl.pallas_call(
        functools.partial(_mla_attn_kernel, tq=tq),
        out_shape=jax.ShapeDtypeStruct((B, S, heads * dv), BF16),
        grid=(B, heads, S // tq),
        in_specs=[pl.BlockSpec((1, tq, 2 * LANE), lambda b, h, i: (b, i, h)),
                  pl.BlockSpec((1, S, 2 * LANE), lambda b, h, i: (b, 0, h)),
                  pl.BlockSpec((1, S, dv), lambda b, h, i: (b, 0, h))],
        out_specs=pl.BlockSpec((1, tq, dv), lambda b, h, i: (b, i, h)),
        compiler_params=_cparams(("arbitrary", "arbitrary", "arbitrary")),
        name="mla_attn",
    )(q, k, v)


def _swa_kernel(sink_ref, q_ref, kp_ref, kc_ref, vp_ref, vc_ref, bias_ref, o_ref, *, groups, per):
    i = pl.program_id(1)
    hd = SWA_HEAD_DIM
    gw = per * hd
    kband = jnp.concatenate([kp_ref[0], kc_ref[0]], axis=0)
    vband = jnp.concatenate([vp_ref[0], vc_ref[0]], axis=0)
    kj = lax.broadcasted_iota(jnp.int32, (BLOCK, 2 * BLOCK), 1)
    has_prev = jnp.logical_or(kj >= BLOCK, i > 0)
    lane = lax.broadcasted_iota(jnp.int32, (BLOCK, gw), 1)
    dn = (((1,), (1,)), ((), ()))
    for g in range(groups):
        kg = kband[:, g * hd:(g + 1) * hd]
        vg = vband[:, g * hd:(g + 1) * hd]
        krep = jnp.concatenate([kg] * per, axis=1)
        vrep = jnp.concatenate([vg] * per, axis=1)
        qg = q_ref[0, :, g * gw:(g + 1) * gw]
        og = jnp.zeros((BLOCK, gw), F32)
        for hh in range(per):
            h = g * per + hh
            sel = jnp.logical_and(lane >= hh * hd, lane < (hh + 1) * hd)
            qm = jnp.where(sel, qg, jnp.zeros_like(qg))
            s = lax.dot_general(qm, krep, dn, preferred_element_type=F32)
            s = s * (hd ** -0.5) + bias_ref[h]
            s = jnp.where(has_prev, s, NEG_BIG)
            sink = sink_ref[h]
            m = jnp.maximum(jnp.max(s, axis=-1, keepdims=True), sink)
            p = jnp.exp(s - m)
            den = jnp.sum(p, axis=-1, keepdims=True) + jnp.exp(sink - m)
            o = jnp.dot(p.astype(vrep.dtype), vrep, preferred_element_type=F32) / den
            og = jnp.where(sel, o, og)
        o_ref[0, :, g * gw:(g + 1) * gw] = og.astype(o_ref.dtype)


def _swa_attn(proj3, lay, sinks, bias_masked):
    B, S, _ = proj3.shape
    nb = S // BLOCK
    groups, per = SWA_KV_HEADS, SWA_Q_HEADS // SWA_KV_HEADS
    qw = SWA_Q_HEADS * SWA_HEAD_DIM
    kw = SWA_KV_HEADS * SWA_HEAD_DIM
    assert lay["q_s"] % qw == 0 and lay["k_s"] % kw == 0 and lay["v_s"] % kw == 0
    qo, ko, vo = lay["q_s"] // qw, lay["k_s"] // kw, lay["v_s"] // kw
    prev = lambda o: (lambda b, i, s: (b, jnp.maximum(i - 1, 0), o))
    cur = lambda o: (lambda b, i, s: (b, i, o))
    grid_spec = pltpu.PrefetchScalarGridSpec(
        num_scalar_prefetch=1,
        grid=(B, nb),
        in_specs=[pl.BlockSpec((1, BLOCK, qw), cur(qo)),
                  pl.BlockSpec((1, BLOCK, kw), prev(ko)),
                  pl.BlockSpec((1, BLOCK, kw), cur(ko)),
                  pl.BlockSpec((1, BLOCK, kw), prev(vo)),
                  pl.BlockSpec((1, BLOCK, kw), cur(vo)),
                  pl.BlockSpec(bias_masked.shape, lambda b, i, s: (0, 0, 0))],
        out_specs=pl.BlockSpec((1, BLOCK, qw), lambda b, i, s: (b, i, 0)),
    )
    return pl.pallas_call(
        functools.partial(_swa_kernel, groups=groups, per=per),
        out_shape=jax.ShapeDtypeStruct((B, S, qw), BF16),
        grid_spec=grid_spec,
        compiler_params=_cparams(("arbitrary", "arbitrary")),
        name="swa_attn",
    )(sinks, proj3, proj3, proj3, proj3, proj3, bias_masked)


def _merge_kernel(om_ref, os_ref, g0_ref, g1_ref, bg_ref, wm_ref, ws_ref, o_ref):
    ym = jnp.dot(om_ref[...], wm_ref[...], preferred_element_type=F32)
    ys = jnp.dot(os_ref[...], ws_ref[...], preferred_element_type=F32)
    g0 = jax.nn.sigmoid(g0_ref[...].astype(F32) + bg_ref[0:1, :])
    g1 = jax.nn.sigmoid(g1_ref[...].astype(F32) + bg_ref[1:2, :])
    o_ref[...] = (g0 * ym + g1 * ys).astype(o_ref.dtype)


def _merge(om, osw, proj, lay, bg, wm, ws):
    T = om.shape[0]
    D = wm.shape[1]
    tm = _tile(T, MERGE_TM)
    assert lay["g0"] % D == 0 and lay["g1"] % D == 0
    row = lambda w: pl.BlockSpec((tm, w), lambda i: (i, 0))
    full = lambda a: pl.BlockSpec(a.shape, lambda i: (0,) * a.ndim)
    return pl.pallas_call(
        _merge_kernel,
        out_shape=jax.ShapeDtypeStruct((T, D), BF16),
        grid=(T // tm,),
        in_specs=[row(om.shape[1]), row(osw.shape[1]),
                  pl.BlockSpec((tm, D), lambda i, o=lay["g0"] // D: (i, o)),
                  pl.BlockSpec((tm, D), lambda i, o=lay["g1"] // D: (i, o)),
                  full(bg), full(wm), full(ws)],
        out_specs=row(D),
        compiler_params=_cparams(("arbitrary",)),
        name="merge",
    )(om, osw, proj, proj, bg, wm, ws)


def _layer_norm(z, g, b):
    mu = jnp.mean(z, axis=-1, keepdims=True)
    zc = z - mu
    var = jnp.mean(zc * zc, axis=-1, keepdims=True)
    return zc * lax.rsqrt(var + LN_EPS) * g + b


def _out_ln_kernel(mg_ref, x_ref, wo_ref, g_ref, b_ref, wr_ref, h_ref, hb_ref, sc_ref):
    mix = jnp.dot(mg_ref[...], wo_ref[...], preferred_element_type=F32)
    h = _layer_norm(ALPHA * x_ref[...] + mix, g_ref[...], b_ref[...])
    h_ref[...] = h
    hb = h.astype(BF16)
    hb_ref[...] = hb
    sc_ref[...] = jax.nn.sigmoid(jnp.dot(hb, wr_ref[...], preferred_element_type=F32))


def _out_ln(merged, x2, wo, g, b, wr):
    T, D = x2.shape
    tm = _tile(T, OUT_TM)
    row = lambda w: pl.BlockSpec((tm, w), lambda i: (i, 0))
    full = lambda a: pl.BlockSpec(a.shape, lambda i: (0,) * a.ndim)
    return pl.pallas_call(
        _out_ln_kernel,
        out_shape=(jax.ShapeDtypeStruct((T, D), F32), jax.ShapeDtypeStruct((T, D), BF16),
                   jax.ShapeDtypeStruct((T, wr.shape[1]), F32)),
        grid=(T // tm,),
        in_specs=[row(D), row(D), full(wo), full(g), full(b), full(wr)],
        out_specs=(row(D), row(D), row(wr.shape[1])),
        compiler_params=_cparams(("arbitrary",)),
        name="out_ln",
    )(merged, x2, wo, g, b, wr)


def _moe_kernel(be_ref, nv_ref, x_ref, wgu_ref, wd_ref, y_ref, *, ff):
    @pl.when(pl.program_id(0) < nv_ref[0])
    def _():
        h = jnp.dot(x_ref[...], wgu_ref[0], preferred_element_type=F32)
        a = (jax.nn.silu(h[:, :ff]) * h[:, ff:]).astype(BF16)
        y_ref[...] = jnp.dot(a, wd_ref[0], preferred_element_type=F32).astype(y_ref.dtype)


def _moe_experts(block_e, nvalid, xs, wgu, wd, tm):
    NP, D = xs.shape
    ff = wd.shape[1]
    grid_spec = pltpu.PrefetchScalarGridSpec(
        num_scalar_prefetch=2,
        grid=(NP // tm,),
        in_specs=[pl.BlockSpec((tm, D), lambda b, be, nv: (b, 0)),
                  pl.BlockSpec((1, D, 2 * ff), lambda b, be, nv: (be[b], 0, 0)),
                  pl.BlockSpec((1, ff, D), lambda b, be, nv: (be[b], 0, 0))],
        out_specs=pl.BlockSpec((tm, D), lambda b, be, nv: (b, 0)),
    )
    return pl.pallas_call(
        functools.partial(_moe_kernel, ff=ff),
        out_shape=jax.ShapeDtypeStruct((NP, D), BF16),
        grid_spec=grid_spec,
        compiler_params=_cparams(("arbitrary",)),
        name="moe_experts",
    )(block_e, nvalid, xs, wgu, wd)


def _combine_kernel(ys_ref, w_ref, h_ref, hb_ref, wsg_ref, wsd_ref, g_ref, b_ref, o_ref, *, ff, topk):
    w = w_ref[...]
    routed = ys_ref[0].astype(F32) * w[:, 0:1]
    for k in range(1, topk):
        routed = routed + ys_ref[k].astype(F32) * w[:, k:k + 1]
    s = jnp.dot(hb_ref[...], wsg_ref[...], preferred_element_type=F32)
    a = (jax.nn.silu(s[:, :ff]) * s[:, ff:]).astype(BF16)
    shared = jnp.dot(a, wsd_ref[...], preferred_element_type=F32)
    o_ref[...] = _layer_norm(ALPHA * h_ref[...] + routed + shared, g_ref[...], b_ref[...])


def _moe_combine(ysel, w, h, hb, wsg, wsd, g, b):
    T, D = h.shape
    ff = wsd.shape[0]
    topk = ysel.shape[0]
    tm = _tile(T, COMBINE_TM)
    row = lambda wd_: pl.BlockSpec((tm, wd_), lambda i: (i, 0))
    full = lambda a: pl.BlockSpec(a.shape, lambda i: (0,) * a.ndim)
    return pl.pallas_call(
        functools.partial(_combine_kernel, ff=ff, topk=topk),
        out_shape=jax.ShapeDtypeStruct((T, D), F32),
        grid=(T // tm,),
        in_specs=[pl.BlockSpec((topk, tm, D), lambda i: (0, i, 0)), row(topk), row(D), row(D),
                  full(wsg), full(wsd), full(g), full(b)],
        out_specs=row(D),
        compiler_params=_cparams(("arbitrary",)),
        name="moe_combine",
    )(ysel, w, h, hb, wsg, wsd, g, b)


def _rot_cols(w):
    half = w.shape[-1] // 2
    return jnp.concatenate([-w[..., half:], w[..., :half]], axis=-1)


def _t5_bucket(dist):
    n = jnp.maximum(dist, 0)
    max_exact = REL_BUCKETS // 2
    large = max_exact + (jnp.log(jnp.maximum(n, 1).astype(F32) / max_exact)
                         / math.log(REL_MAX_DIST / max_exact) * (REL_BUCKETS - max_exact)).astype(jnp.int32)
    large = jnp.minimum(large, REL_BUCKETS - 1)
    return jnp.where(n < max_exact, n, large)


def _layout(widths):
    order = sorted(range(len(widths)), key=lambda j: -widths[j][1])
    lay, off = {}, 0
    for j in order:
        name, w = widths[j]
        assert off % w == 0, (name, off, w)
        lay[name] = off
        off += w
    return lay, off, [widths[j][0] for j in order]


def _route(scores, block):
    T = scores.shape[0]
    sel_bias = scores
    grp_score = lax.top_k(sel_bias.reshape(T, N_GROUPS, -1), 2)[0].sum(-1)
    _, top_g = lax.top_k(grp_score, TOPK_GROUPS)
    gmask = jax.nn.one_hot(top_g, N_GROUPS).sum(-2) > 0
    emask = jnp.repeat(gmask, N_EXPERTS // N_GROUPS, axis=-1)
    return emask


def kernel(x, positions, w_in, b_gate, q_norm_g, kv_norm_g, w_uq, w_uk, w_uv, swa_sinks, rel_table,
           w_br_mla, w_br_swa, w_out, ln1_g, ln1_b, w_router, router_bias, w_gate_up, w_down,
           w_shared_gate_up, w_shared_down, ln2_g, ln2_b):
    B, S, D = x.shape
    T = B * S
    H = MLA_HEADS
    assert w_in.shape[0] == DEPTH == 1
    qr, kvr = w_uq.shape[1], w_uk.shape[1]
    rd = MLA_ROPE_DIM
    qw = SWA_Q_HEADS * SWA_HEAD_DIM
    kw = SWA_KV_HEADS * SWA_HEAD_DIM

    sizes = (qr, kvr, rd, qw, kw, kw, N_BRANCHES * D)
    bounds = np.cumsum(sizes)[:-1].tolist()
    w_cq, w_ckv, w_kr, w_qs, w_ks, w_vs, w_gate = jnp.split(w_in[0], bounds, axis=-1)
    zpad = jnp.zeros((D, LANE - rd), F32)
    segs = {"g0": w_gate[:, :D], "g1": w_gate[:, D:], "q_s": w_qs, "c_q": w_cq, "c_kv": w_ckv,
            "k_s": w_ks, "v_s": w_vs,
            "kra": jnp.concatenate([w_kr, zpad], axis=1),
            "krb": jnp.concatenate([_rot_cols(w_kr), zpad], axis=1)}
    lay, total, order = _layout([(n, int(a.shape[1])) for n, a in segs.items()])
    w_p = jnp.concatenate([segs[n] for n in order], axis=1).astype(BF16)

    wq = w_uq[0].reshape(qr, H, MLA_NOPE_DIM + rd)
    wq_nope, wq_rope = wq[..., :MLA_NOPE_DIM], wq[..., MLA_NOPE_DIM:]
    z64 = jnp.zeros((qr, H, LANE - rd), F32)
    wqa = jnp.concatenate([wq_nope, wq_rope, z64], axis=-1).reshape(qr, H * 2 * LANE).astype(BF16)
    wqb = jnp.concatenate([_rot_cols(wq_rope), z64], axis=-1).reshape(qr, H * LANE).astype(BF16)
    wuk = w_uk[0].astype(BF16)
    wuv = w_uv[0].astype(BF16)

    inv = ROPE_THETA ** (-jnp.arange(0, rd, 2, dtype=F32) / rd)
    ang = positions.astype(F32).reshape(T, 1) * inv
    zc = jnp.zeros((T, LANE - rd), F32)
    cosp = jnp.concatenate([jnp.cos(ang), jnp.cos(ang), zc], axis=1)
    sinp = jnp.concatenate([jnp.sin(ang), jnp.sin(ang), zc], axis=1)

    qi = jnp.arange(BLOCK)[:, None]
    kj = jnp.arange(2 * BLOCK)[None, :]
    dist = qi + BLOCK - kj
    bias = rel_table[_t5_bucket(dist)].astype(F32).transpose(2, 0, 1)
    in_window = (dist >= 0) & (dist < WINDOW)
    bias_masked = jnp.where(in_window[None], bias, NEG_BIG)

    x2 = x.reshape(T, D)

    proj = _in_proj(x2, w_p)
    q, k, v = _mla_proj(proj, lay, cosp, sinp, q_norm_g[0].reshape(1, qr), kv_norm_g[0].reshape(1, kvr),
                        wqa, wqb, wuk, wuv, H)
    o_m = _mla_attn(q.reshape(B, S, -1), k.reshape(B, S, -1), v.reshape(B, S, -1), H).reshape(T, -1)
    o_s = _swa_attn(proj.reshape(B, S, total), lay, swa_sinks[0].astype(F32), bias_masked).reshape(T, qw)
    merged = _merge(o_m, o_s, proj, lay, b_gate[0], w_br_mla[0].astype(BF16), w_br_swa[0].astype(BF16))
    wr = jnp.concatenate([w_router[0], jnp.zeros((D, LANE - N_EXPERTS), F32)], axis=1).astype(BF16)
    h, hb, sc = _out_ln(merged, x2, w_out[0].astype(BF16), ln1_g[0].reshape(1, D), ln1_b[0].reshape(1, D), wr)

    scores = sc[:, :N_EXPERTS]
    sel = scores + router_bias[0].astype(F32)
    grp_score = lax.top_k(sel.reshape(T, N_GROUPS, -1), 2)[0].sum(-1)
    _, top_g = lax.top_k(grp_score, TOPK_GROUPS)
    gmask = jax.nn.one_hot(top_g, N_GROUPS).sum(-2) > 0
    emask = jnp.repeat(gmask, N_EXPERTS // N_GROUPS, axis=-1)
    _, top_e = lax.top_k(jnp.where(emask, sel, -jnp.inf), TOP_K)
    wts = jnp.take_along_axis(scores, top_e, axis=-1)
    wts = wts / jnp.sum(wts, -1, keepdims=True) * ROUTED_SCALE

    tm = MOE_TM
    N = T * TOP_K
    NB = -(-N // tm) + N_EXPERTS
    NP = NB * tm
    flat_e = top_e.reshape(-1)
    order_idx = jnp.argsort(flat_e)
    sorted_e = flat_e[order_idx]
    gs = jnp.bincount(flat_e, length=N_EXPERTS)
    ps = (gs + tm - 1) // tm * tm
    start = jnp.cumsum(gs) - gs
    pend = jnp.cumsum(ps)
    pstart = pend - ps
    dest = (pstart[sorted_e] + jnp.arange(N) - start[sorted_e]).astype(jnp.int32)
    row_tok = jnp.zeros((NP,), jnp.int32).at[dest].set((order_idx // TOP_K).astype(jnp.int32))
    pos = jnp.zeros((N,), jnp.int32).at[order_idx].set(dest).reshape(T, TOP_K)
    block_e = jnp.minimum(jnp.searchsorted(pend, jnp.arange(NB) * tm, side='right'),
                          N_EXPERTS - 1).astype(jnp.int32)
    nvalid = (pend[-1] // tm).astype(jnp.int32).reshape(1)

    xs = hb[row_tok]
    ys = _moe_experts(block_e, nvalid, xs, w_gate_up[0].astype(BF16), w_down[0].astype(BF16), tm)
    ysel = ys[pos.T]
    out = _moe_combine(ysel, wts, h, hb, w_shared_gate_up[0].astype(BF16), w_shared_down[0].astype(BF16),
                       ln2_g[0].reshape(1, D), ln2_b[0].reshape(1, D))
    return out.reshape(B, S, D)
```

```python
import functools
import math

import jax
import jax.numpy as jnp
import numpy as np
from jax import lax
from jax.experimental import pallas as pl
from jax.experimental.pallas import tpu as pltpu

F32 = jnp.float32
BF16 = jnp.bfloat16

MLA_HEADS = 8
MLA_NOPE_DIM = 128
MLA_ROPE_DIM = 64
MLA_V_DIM = 128
ROPE_THETA = 10000.0
SWA_Q_HEADS = 16
SWA_KV_HEADS = 4
SWA_HEAD_DIM = 64
WINDOW = 128
BLOCK = 128
REL_BUCKETS = 32
REL_MAX_DIST = 128
N_BRANCHES = 2
N_EXPERTS = 64
TOP_K = 8
N_GROUPS = 8
TOPK_GROUPS = 4
ROUTED_SCALE = 2.5
DEPTH = 1
ALPHA = (2 * DEPTH) ** 0.25
LN_EPS = 1e-5
RMS_EPS = 1e-6

LANE = 128
NEG_BIG = -1e30

PROJ_TM = 512
MLA_PROJ_TM = 512
MLA_TQ = 512
MERGE_TM = 512
OUT_TM = 512
MOE_TM = 256
COMBINE_TM = 256
ROUTER_TM = 512
DISPATCH_TM = 256
VMEM_LIMIT = 56 * 1024 * 1024


def _cparams(sem):
    return pltpu.CompilerParams(dimension_semantics=sem, vmem_limit_bytes=VMEM_LIMIT)


def _tile(n, t):
    t = min(n, t)
    assert n % t == 0, (n, t)
    return t


def _in_proj_kernel(x_ref, w_ref, o_ref):
    x = x_ref[...].astype(BF16)
    o_ref[...] = jnp.dot(x, w_ref[...], preferred_element_type=F32).astype(o_ref.dtype)


def _in_proj(x2, w_p):
    T, D = x2.shape
    N = w_p.shape[1]
    tn = N // 2 if (N // 2) % LANE == 0 and N % 2 == 0 else N
    tm = _tile(T, PROJ_TM)
    return pl.pallas_call(
        _in_proj_kernel,
        out_shape=jax.ShapeDtypeStruct((T, N), BF16),
        grid=(N // tn, T // tm),
        in_specs=[pl.BlockSpec((tm, D), lambda j, i: (i, 0)),
                  pl.BlockSpec((D, tn), lambda j, i: (0, j))],
        out_specs=pl.BlockSpec((tm, tn), lambda j, i: (i, j)),
        compiler_params=_cparams(("arbitrary", "arbitrary")),
        name="in_proj",
    )(x2, w_p)


def _mla_proj_kernel(cq_ref, ckv_ref, kra_ref, krb_ref, cos_ref, sin_ref, qg_ref, kvg_ref,
                     wqa_ref, wqb_ref, wuk_ref, wuv_ref, q_ref, k_ref, v_ref, *, heads, scale):
    cos = cos_ref[...]
    sin = sin_ref[...]
    cq = cq_ref[...].astype(F32)
    qn = cq * lax.rsqrt(jnp.mean(cq * cq, axis=-1, keepdims=True) + RMS_EPS) * qg_ref[...]
    qn = qn.astype(BF16)
    qa = jnp.dot(qn, wqa_ref[...], preferred_element_type=F32)
    qb = jnp.dot(qn, wqb_ref[...], preferred_element_type=F32)
    ckv = ckv_ref[...].astype(F32)
    cn = ckv * lax.rsqrt(jnp.mean(ckv * ckv, axis=-1, keepdims=True) + RMS_EPS) * kvg_ref[...]
    cn = cn.astype(BF16)
    kn = jnp.dot(cn, wuk_ref[...], preferred_element_type=F32)
    v_ref[...] = jnp.dot(cn, wuv_ref[...], preferred_element_type=F32).astype(v_ref.dtype)
    krope = (kra_ref[...].astype(F32) * cos + krb_ref[...].astype(F32) * sin).astype(k_ref.dtype)
    for h in range(heads):
        lo = h * 2 * LANE
        q_ref[:, lo:lo + LANE] = (qa[:, lo:lo + LANE] * scale).astype(q_ref.dtype)
        q_ref[:, lo + LANE:lo + 2 * LANE] = (
            (qa[:, lo + LANE:lo + 2 * LANE] * cos + qb[:, h * LANE:(h + 1) * LANE] * sin) * scale
        ).astype(q_ref.dtype)
        k_ref[:, lo:lo + LANE] = kn[:, h * LANE:(h + 1) * LANE].astype(k_ref.dtype)
        k_ref[:, lo + LANE:lo + 2 * LANE] = krope


def _mla_proj(proj, lay, cosp, sinp, qg, kvg, wqa, wqb, wuk, wuv, heads):
    T = proj.shape[0]
    tm = _tile(T, MLA_PROJ_TM)
    qr, kvr = wqa.shape[0], wuk.shape[0]

    def col(name, width):
        off = lay[name]
        assert off % width == 0
        return pl.BlockSpec((tm, width), lambda i, o=off // width: (i, o))

    row = lambda w: pl.BlockSpec((tm, w), lambda i: (i, 0))
    full = lambda a: pl.BlockSpec(a.shape, lambda i: (0,) * a.ndim)
    scale = (MLA_NOPE_DIM + MLA_ROPE_DIM) ** -0.5
    return pl.pallas_call(
        functools.partial(_mla_proj_kernel, heads=heads, scale=scale),
        out_shape=(jax.ShapeDtypeStruct((T, heads * 2 * LANE), BF16),
                   jax.ShapeDtypeStruct((T, heads * 2 * LANE), BF16),
                   jax.ShapeDtypeStruct((T, heads * MLA_V_DIM), BF16)),
        grid=(T // tm,),
        in_specs=[col("c_q", qr), col("c_kv", kvr), col("kra", LANE), col("krb", LANE),
                  row(LANE), row(LANE), full(qg), full(kvg), full(wqa), full(wqb), full(wuk), full(wuv)],
        out_specs=(row(heads * 2 * LANE), row(heads * 2 * LANE), row(heads * MLA_V_DIM)),
        compiler_params=_cparams(("arbitrary",)),
        name="mla_proj",
    )(proj, proj, proj, proj, cosp, sinp, qg, kvg, wqa, wqb, wuk, wuv)


def _mla_attn_kernel(q_ref, k_ref, v_ref, o_ref, *, tq):
    i = pl.program_id(2)
    q = q_ref[0]
    dn = (((1,), (1,)), ((), ()))

    def step(kb, vb, carry, mask):
        m, l, acc = carry
        s = lax.dot_general(q, kb, dn, preferred_element_type=F32)
        if mask is not None:
            s = jnp.where(mask, s, NEG_BIG)
        m_new = jnp.maximum(m, jnp.max(s, axis=-1, keepdims=True))
        p = jnp.exp(s - m_new)
        a = jnp.exp(m - m_new)
        l = a * l + jnp.sum(p, axis=-1, keepdims=True)
        acc = a * acc + jnp.dot(p.astype(vb.dtype), vb, preferred_element_type=F32)
        return m_new, l, acc

    def body(j, carry):
        off = pl.multiple_of(j * tq, tq)
        return step(k_ref[0, pl.ds(off, tq), :], v_ref[0, pl.ds(off, tq), :], carry, None)

    init = (jnp.full((tq, 1), NEG_BIG, F32), jnp.zeros((tq, 1), F32),
            jnp.zeros((tq, v_ref.shape[-1]), F32))
    carry = lax.fori_loop(0, i, body, init)
    off = pl.multiple_of(i * tq, tq)
    r = lax.broadcasted_iota(jnp.int32, (tq, tq), 0)
    c = lax.broadcasted_iota(jnp.int32, (tq, tq), 1)
    m, l, acc = step(k_ref[0, pl.ds(off, tq), :], v_ref[0, pl.ds(off, tq), :], carry, c <= r)
    o_ref[0] = (acc / l).astype(o_ref.dtype)


def _mla_attn(q, k, v, heads):
    B, S, _ = q.shape
    tq = _tile(S, MLA_TQ)
    dv = v.shape[-1] // heads
    return pl.pallas_call(
        functools.partial(_mla_attn_kernel, tq=tq),
        out_shape=jax.ShapeDtypeStruct((B, S, heads * dv), BF16),
        grid=(B, heads, S // tq),
        in_specs=[pl.BlockSpec((1, tq, 2 * LANE), lambda b, h, i: (b, i, h)),
                  pl.BlockSpec((1, S, 2 * LANE), lambda b, h, i: (b, 0, h)),
                  pl.BlockSpec((1, S, dv), lambda b, h, i: (b, 0, h))],
        out_specs=pl.BlockSpec((1, tq, dv), lambda b, h, i: (b, i, h)),
        compiler_params=_cparams(("arbitrary", "arbitrary", "arbitrary")),
        name="mla_attn",
    )(q, k, v)


def _swa_kernel(sink_ref, q_ref, kp_ref, kc_ref, vp_ref, vc_ref, bias_ref, o_ref, *, groups, per):
    i = pl.program_id(1)
    hd = SWA_HEAD_DIM
    gw = per * hd
    kband = jnp.concatenate([kp_ref[0], kc_ref[0]], axis=0)
    vband = jnp.concatenate([vp_ref[0], vc_ref[0]], axis=0)
    kj = lax.broadcasted_iota(jnp.int32, (BLOCK, 2 * BLOCK), 1)
    has_prev = jnp.logical_or(kj >= BLOCK, i > 0)
    lane = lax.broadcasted_iota(jnp.int32, (BLOCK, gw), 1)
    dn = (((1,), (1,)), ((), ()))
    for g in range(groups):
        kg = kband[:, g * hd:(g + 1) * hd]
        vg = vband[:, g * hd:(g + 1) * hd]
        krep = jnp.concatenate([kg] * per, axis=1)
        vrep = jnp.concatenate([vg] * per, axis=1)
        qg = q_ref[0, :, g * gw:(g + 1) * gw]
        og = jnp.zeros((BLOCK, gw), F32)
        for hh in range(per):
            h = g * per + hh
            sel = jnp.logical_and(lane >= hh * hd, lane < (hh + 1) * hd)
            qm = jnp.where(sel, qg, jnp.zeros_like(qg))
            s = lax.dot_general(qm, krep, dn, preferred_element_type=F32)
            s = s * (hd ** -0.5) + bias_ref[h]
            s = jnp.where(has_prev, s, NEG_BIG)
            sink = sink_ref[h]
            m = jnp.maximum(jnp.max(s, axis=-1, keepdims=True), sink)
            p = jnp.exp(s - m)
            den = jnp.sum(p, axis=-1, keepdims=True) + jnp.exp(sink - m)
            o = jnp.dot(p.astype(vrep.dtype), vrep, preferred_element_type=F32) / den
            og = jnp.where(sel, o, og)
        o_ref[0, :, g * gw:(g + 1) * gw] = og.astype(o_ref.dtype)


def _swa_attn(proj3, lay, sinks, bias_masked):
    B, S, _ = proj3.shape
    nb = S // BLOCK
    groups, per = SWA_KV_HEADS, SWA_Q_HEADS // SWA_KV_HEADS
    qw = SWA_Q_HEADS * SWA_HEAD_DIM
    kw = SWA_KV_HEADS * SWA_HEAD_DIM
    assert lay["q_s"] % qw == 0 and lay["k_s"] % kw == 0 and lay["v_s"] % kw == 0
    qo, ko, vo = lay["q_s"] // qw, lay["k_s"] // kw, lay["v_s"] // kw
    prev = lambda o: (lambda b, i, s: (b, jnp.maximum(i - 1, 0), o))
    cur = lambda o: (lambda b, i, s: (b, i, o))
    grid_spec = pltpu.PrefetchScalarGridSpec(
        num_scalar_prefetch=1,
        grid=(B, nb),
        in_specs=[pl.BlockSpec((1, BLOCK, qw), cur(qo)),
                  pl.BlockSpec((1, BLOCK, kw), prev(ko)),
                  pl.BlockSpec((1, BLOCK, kw), cur(ko)),
                  pl.BlockSpec((1, BLOCK, kw), prev(vo)),
                  pl.BlockSpec((1, BLOCK, kw), cur(vo)),
                  pl.BlockSpec(bias_masked.shape, lambda b, i, s: (0, 0, 0))],
        out_specs=pl.BlockSpec((1, BLOCK, qw), lambda b, i, s: (b, i, 0)),
    )
    return pl.pallas_call(
        functools.partial(_swa_kernel, groups=groups, per=per),
        out_shape=jax.ShapeDtypeStruct((B, S, qw), BF16),
        grid_spec=grid_spec,
        compiler_params=_cparams(("arbitrary", "arbitrary")),
        name="swa_attn",
    )(sinks, proj3, proj3, proj3, proj3, proj3, bias_masked)


def _merge_kernel(om_ref, os_ref, g0_ref, g1_ref, bg_ref, wm_ref, ws_ref, o_ref):
    ym = jnp.dot(om_ref[...], wm_ref[...], preferred_element_type=F32)
    ys = jnp.dot(os_ref[...], ws_ref[...], preferred_element_type=F32)
    g0 = jax.nn.sigmoid(g0_ref[...].astype(F32) + bg_ref[0:1, :])
    g1 = jax.nn.sigmoid(g1_ref[...].astype(F32) + bg_ref[1:2, :])
    o_ref[...] = (g0 * ym + g1 * ys).astype(o_ref.dtype)


def _merge(om, osw, proj, lay, bg, wm, ws):
    T = om.shape[0]
    D = wm.shape[1]
    tm = _tile(T, MERGE_TM)
    assert lay["g0"] % D == 0 and lay["g1"] % D == 0
    row = lambda w: pl.BlockSpec((tm, w), lambda i: (i, 0))
    full = lambda a: pl.BlockSpec(a.shape, lambda i: (0,) * a.ndim)
    return pl.pallas_call(
        _merge_kernel,
        out_shape=jax.ShapeDtypeStruct((T, D), BF16),
        grid=(T // tm,),
        in_specs=[row(om.shape[1]), row(osw.shape[1]),
                  pl.BlockSpec((tm, D), lambda i, o=lay["g0"] // D: (i, o)),
                  pl.BlockSpec((tm, D), lambda i, o=lay["g1"] // D: (i, o)),
                  full(bg), full(wm), full(ws)],
        out_specs=row(D),
        compiler_params=_cparams(("arbitrary",)),
        name="merge",
    )(om, osw, proj, proj, bg, wm, ws)


def _layer_norm(z, g, b):
    mu = jnp.mean(z, axis=-1, keepdims=True)
    zc = z - mu
    var = jnp.mean(zc * zc, axis=-1, keepdims=True)
    return zc * lax.rsqrt(var + LN_EPS) * g + b


def _pack_pairs(y):
    n = y.shape[1] // 2
    lo = pltpu.bitcast(y[:, :n].astype(BF16).astype(F32), jnp.uint32)
    hi = pltpu.bitcast(y[:, n:].astype(BF16).astype(F32), jnp.uint32)
    return (lo >> 16) | (hi & jnp.uint32(0xFFFF0000))


def _unpack_pairs(w):
    lo = pltpu.bitcast(w << 16, F32)
    hi = pltpu.bitcast(w & jnp.uint32(0xFFFF0000), F32)
    return lo, hi


def _out_ln_kernel(mg_ref, x_ref, wo_ref, g_ref, b_ref, wrt_ref, h_ref, hb_ref, hp_ref, st_ref):
    mix = jnp.dot(mg_ref[...], wo_ref[...], preferred_element_type=F32)
    h = _layer_norm(ALPHA * x_ref[...] + mix, g_ref[...], b_ref[...])
    h_ref[...] = h
    hb = h.astype(BF16)
    hb_ref[...] = hb
    hp_ref[...] = _pack_pairs(h)
    logits_t = lax.dot_general(wrt_ref[...], hb, (((1,), (1,)), ((), ())), preferred_element_type=F32)
    st_ref[...] = jax.nn.sigmoid(logits_t)


def _out_ln(merged, x2, wo, g, b, wrt):
    T, D = x2.shape
    E = wrt.shape[0]
    tm = _tile(T, OUT_TM)
    row = lambda w: pl.BlockSpec((tm, w), lambda i: (i, 0))
    full = lambda a: pl.BlockSpec(a.shape, lambda i: (0,) * a.ndim)
    return pl.pallas_call(
        _out_ln_kernel,
        out_shape=(jax.ShapeDtypeStruct((T, D), F32), jax.ShapeDtypeStruct((T, D), BF16),
                   jax.ShapeDtypeStruct((T, D // 2), jnp.uint32), jax.ShapeDtypeStruct((E, T), F32)),
        grid=(T // tm,),
        in_specs=[row(D), row(D), full(wo), full(g), full(b), full(wrt)],
        out_specs=(row(D), row(D), row(D // 2), pl.BlockSpec((E, tm), lambda i: (0, i))),
        compiler_params=_cparams(("arbitrary",)),
        name="out_ln",
    )(merged, x2, wo, g, b, wrt)


def _router_kernel(s_ref, b_ref, u_ref, te_ref, w_ref, rk_ref, cnt_ref, carry_ref, *, per):
    G = N_GROUPS
    tm = s_ref.shape[1]
    ninf = -jnp.inf

    @pl.when(pl.program_id(0) == 0)
    def _():
        carry_ref[...] = jnp.zeros_like(carry_ref)

    giota = lax.broadcasted_iota(jnp.int32, (G, tm), 0)
    s = [s_ref[j * G:(j + 1) * G, :] for j in range(per)]
    sel = [s[j] + b_ref[j * G:(j + 1) * G, :] for j in range(per)]
    eidx = [giota * per + j for j in range(per)]

    m1 = functools.reduce(jnp.maximum, sel)
    j1 = functools.reduce(jnp.minimum, [jnp.where(sel[j] == m1, j, per) for j in range(per)])
    m2 = functools.reduce(jnp.maximum, [jnp.where(j1 == j, ninf, sel[j]) for j in range(per)])
    grp = m1 + m2
    gmask = jnp.zeros((G, tm), jnp.bool_)
    for _ in range(TOPK_GROUPS):
        mx = jnp.max(grp, axis=0, keepdims=True)
        gi = jnp.min(jnp.where(grp == mx, giota, G), axis=0, keepdims=True)
        chosen = giota == gi
        gmask = jnp.logical_or(gmask, chosen)
        grp = jnp.where(chosen, ninf, grp)

    cur = [jnp.where(gmask, sel[j], ninf) for j in range(per)]
    mem = [jnp.zeros((G, tm), F32) for _ in range(per)]
    e_rows, w_rows = [], []
    for _ in range(TOP_K):
        mx = jnp.max(functools.reduce(jnp.maximum, cur), axis=0, keepdims=True)
        cand = functools.reduce(jnp.minimum, [jnp.where(cur[j] == mx, eidx[j], N_EXPERTS) for j in range(per)])
        emin = jnp.min(cand, axis=0, keepdims=True)
        wk = jnp.zeros((1, tm), F32)
        for j in range(per):
            hit = eidx[j] == emin
            wk = wk + jnp.sum(jnp.where(hit, s[j], 0.0), axis=0, keepdims=True)
            mem[j] = jnp.where(hit, 1.0, mem[j])
            cur[j] = jnp.where(hit, ninf, cur[j])
        e_rows.append(emin)
        w_rows.append(wk)
    wsum = functools.reduce(lambda a, b: a + b, w_rows)

    memall = jnp.concatenate(mem, axis=0)
    pre = jnp.dot(memall.astype(BF16), u_ref[...], preferred_element_type=F32)
    rank = pre + carry_ref[:, 0:1]
    tot = carry_ref[...] + jnp.sum(memall, axis=1, keepdims=True)
    carry_ref[...] = tot
    cnt_ref[...] = tot

    kiota = lax.broadcasted_iota(jnp.int32, (TOP_K, tm), 0)
    te = jnp.zeros((TOP_K, tm), jnp.int32)
    wt = jnp.zeros((TOP_K, tm), F32)
    rk = jnp.zeros((TOP_K, tm), F32)
    for k in range(TOP_K):
        rk_k = jnp.zeros((1, tm), F32)
        for j in range(per):
            rk_k = rk_k + jnp.sum(jnp.where(eidx[j] == e_rows[k], rank[j * G:(j + 1) * G, :], 0.0),
                                  axis=0, keepdims=True)
        te = jnp.where(kiota == k, e_rows[k], te)
        wt = jnp.where(kiota == k, w_rows[k] / wsum * ROUTED_SCALE, wt)
        rk = jnp.where(kiota == k, rk_k, rk)
    te_ref[...] = te
    w_ref[...] = wt
    rk_ref[...] = rk.astype(jnp.int32)


def _router(scores_t, bias_col, utri):
    E, T = scores_t.shape
    tm = utri.shape[0]
    per = E // N_GROUPS
    tk = lambda: pl.BlockSpec((TOP_K, tm), lambda i: (0, i))
    return pl.pallas_call(
        functools.partial(_router_kernel, per=per),
        out_shape=(jax.ShapeDtypeStruct((TOP_K, T), jnp.int32), jax.ShapeDtypeStruct((TOP_K, T), F32),
                   jax.ShapeDtypeStruct((TOP_K, T), jnp.int32), jax.ShapeDtypeStruct((E, LANE), F32)),
        grid=(T // tm,),
        in_specs=[pl.BlockSpec((E, tm), lambda i: (0, i)),
                  pl.BlockSpec((E, 1), lambda i: (0, 0)),
                  pl.BlockSpec((tm, tm), lambda i: (0, 0))],
        out_specs=(tk(), tk(), tk(), pl.BlockSpec((E, LANE), lambda i: (0, 0))),
        scratch_shapes=[pltpu.VMEM((E, LANE), F32)],
        compiler_params=_cparams(("arbitrary",)),
        name="router",
    )(scores_t, bias_col, utri)


def _dispatch_kernel(dest_ref, h_ref, xs_ref, sem):
    tm = h_ref.shape[0]

    def body(r, c):
        for k in range(TOP_K):
            pltpu.make_async_copy(h_ref.at[pl.ds(r, 1)], xs_ref.at[pl.ds(dest_ref[k, r], 1)], sem).start()
        return c

    lax.fori_loop(0, tm, body, 0)
    for k in range(TOP_K):
        pltpu.make_async_copy(h_ref, xs_ref.at[pl.ds(0, tm)], sem).wait()


def _dispatch(dest_t, hp):
    T, W = hp.shape
    tm = _tile(T, DISPATCH_TM)
    return pl.pallas_call(
        _dispatch_kernel,
        out_shape=jax.ShapeDtypeStruct((T * TOP_K, W), hp.dtype),
        grid=(T // tm,),
        in_specs=[pl.BlockSpec((TOP_K, tm), lambda i: (0, i), memory_space=pltpu.SMEM),
                  pl.BlockSpec((tm, W), lambda i: (i, 0))],
        out_specs=pl.BlockSpec(memory_space=pl.ANY),
        scratch_shapes=[pltpu.SemaphoreType.DMA(())],
        compiler_params=_cparams(("arbitrary",)),
        name="dispatch",
    )(dest_t, hp)


def _moe_kernel(ib_ref, ie_ref, lo_ref, hi_ref, n_ref, x_ref, wgu_ref, wd_ref, y_ref, acc_ref, *, ff):
    i = pl.program_id(0)
    n = n_ref[0]
    last_i = ib_ref.shape[0] - 1
    tm = x_ref.shape[0]
    half = x_ref.shape[1]

    @pl.when(i < n)
    def _():
        b = ib_ref[i]
        first = jnp.logical_or(i == 0, ib_ref[jnp.maximum(i - 1, 0)] != b)
        last = jnp.logical_or(i == n - 1, ib_ref[jnp.minimum(i + 1, last_i)] != b)
        xlo, xhi = _unpack_pairs(x_ref[...])
        h = (jnp.dot(xlo.astype(BF16), wgu_ref[0, :half, :], preferred_element_type=F32)
             + jnp.dot(xhi.astype(BF16), wgu_ref[0, half:, :], preferred_element_type=F32))
        a = (jax.nn.silu(h[:, :ff]) * h[:, ff:]).astype(BF16)
        y = jnp.dot(a, wd_ref[0], preferred_element_type=F32)
        rows = lax.broadcasted_iota(jnp.int32, (tm, 1), 0)
        keep = jnp.logical_and(rows >= lo_ref[i], rows < hi_ref[i])
        y = jnp.where(keep, y, 0.0)

        @pl.when(first)
        def _():
            acc_ref[...] = y

        @pl.when(jnp.logical_not(first))
        def _():
            acc_ref[...] += y

        @pl.when(last)
        def _():
            y_ref[...] = _pack_pairs(acc_ref[...])


def _moe_experts(ib, ie, lo, hi, n_items, xs, wgu, wd, tm):
    N, W = xs.shape
    ff = wd.shape[1]
    D = wd.shape[2]
    grid_spec = pltpu.PrefetchScalarGridSpec(
        num_scalar_prefetch=5,
        grid=(ib.shape[0],),
        in_specs=[pl.BlockSpec((tm, W), lambda i, ib, ie, lo, hi, n: (ib[i], 0)),
                  pl.BlockSpec((1, D, 2 * ff), lambda i, ib, ie, lo, hi, n: (ie[i], 0, 0)),
                  pl.BlockSpec((1, ff, D), lambda i, ib, ie, lo, hi, n: (ie[i], 0, 0))],
        out_specs=pl.BlockSpec((tm, W), lambda i, ib, ie, lo, hi, n: (ib[i], 0)),
        scratch_shapes=[pltpu.VMEM((tm, D), F32)],
    )
    return pl.pallas_call(
        functools.partial(_moe_kernel, ff=ff),
        out_shape=jax.ShapeDtypeStruct((N, W), jnp.uint32),
        grid_spec=grid_spec,
        compiler_params=_cparams(("arbitrary",)),
        name="moe_experts",
    )(ib, ie, lo, hi, n_items, xs, wgu, wd)


def _combine_kernel(dest_ref, ys_ref, w_ref, h_ref, hb_ref, wsg_ref, wsd_ref, g_ref, b_ref, o_ref,
                    ybuf, sem, *, ff):
    tm = h_ref.shape[0]

    def body(r, c):
        for k in range(TOP_K):
            pltpu.make_async_copy(ys_ref.at[pl.ds(dest_ref[k, r], 1)], ybuf.at[k, pl.ds(r, 1)], sem).start()
        return c

    lax.fori_loop(0, tm, body, 0)
    s = jnp.dot(hb_ref[...], wsg_ref[...], preferred_element_type=F32)
    a = (jax.nn.silu(s[:, :ff]) * s[:, ff:]).astype(BF16)
    shared = jnp.dot(a, wsd_ref[...], preferred_element_type=F32)
    for k in range(TOP_K):
        pltpu.make_async_copy(ys_ref.at[pl.ds(0, tm)], ybuf.at[k], sem).wait()
    w = w_ref[...]
    rlo = jnp.zeros((tm, ybuf.shape[2]), F32)
    rhi = jnp.zeros((tm, ybuf.shape[2]), F32)
    for k in range(TOP_K):
        lo, hi = _unpack_pairs(ybuf[k])
        rlo = rlo + lo * w[:, k:k + 1]
        rhi = rhi + hi * w[:, k:k + 1]
    routed = jnp.concatenate([rlo, rhi], axis=1)
    o_ref[...] = _layer_norm(ALPHA * h_ref[...] + routed + shared, g_ref[...], b_ref[...])


def _moe_combine(dest_t, ys, w, h, hb, wsg, wsd, g, b):
    T, D = h.shape
    ff = wsd.shape[0]
    tm = _tile(T, COMBINE_TM)
    row = lambda wd_: pl.BlockSpec((tm, wd_), lambda i: (i, 0))
    full = lambda a: pl.BlockSpec(a.shape, lambda i: (0,) * a.ndim)
    return pl.pallas_call(
        functools.partial(_combine_kernel, ff=ff),
        out_shape=jax.ShapeDtypeStruct((T, D), F32),
        grid=(T // tm,),
        in_specs=[pl.BlockSpec((TOP_K, tm), lambda i: (0, i), memory_space=pltpu.SMEM),
                  pl.BlockSpec(memory_space=pl.ANY), row(TOP_K), row(D), row(D),
                  full(wsg), full(wsd), full(g), full(b)],
        out_specs=row(D),
        scratch_shapes=[pltpu.VMEM((TOP_K, tm, ys.shape[1]), ys.dtype), pltpu.SemaphoreType.DMA(())],
        compiler_params=_cparams(("arbitrary",)),
        name="moe_combine",
    )(dest_t, ys, w, h, hb, wsg, wsd, g, b)


def _rot_cols(w):
    half = w.shape[-1] // 2
    return jnp.concatenate([-w[..., half:], w[..., :half]], axis=-1)


def _t5_bucket(dist):
    n = jnp.maximum(dist, 0)
    max_exact = REL_BUCKETS // 2
    large = max_exact + (jnp.log(jnp.maximum(n, 1).astype(F32) / max_exact)
                         / math.log(REL_MAX_DIST / max_exact) * (REL_BUCKETS - max_exact)).astype(jnp.int32)
    large = jnp.minimum(large, REL_BUCKETS - 1)
    return jnp.where(n < max_exact, n, large)


def _layout(widths):
    order = sorted(range(len(widths)), key=lambda j: -widths[j][1])
    lay, off = {}, 0
    for j in order:
        name, w = widths[j]
        assert off % w == 0, (name, off, w)
        lay[name] = off
        off += w
    return lay, off, [widths[j][0] for j in order]


def _work_items(gs, tm, n_blocks):
    E = gs.shape[0]
    end = jnp.cumsum(gs)
    start = end - gs
    b0 = start // tm
    n_e = jnp.where(gs > 0, (end + tm - 1) // tm - b0, 0)
    icum = jnp.cumsum(n_e)
    n_items = icum[-1]
    cap = n_blocks + E
    idx = jnp.arange(cap, dtype=jnp.int32)
    ie = jnp.minimum(jnp.sum(idx[:, None] >= icum[None, :], axis=1), E - 1).astype(jnp.int32)
    ib = (b0[ie] + idx - (icum - n_e)[ie]).astype(jnp.int32)
    lo = jnp.maximum(start[ie] - ib * tm, 0)
    hi = jnp.minimum(end[ie] - ib * tm, tm)
    ok = idx < n_items
    ib = jnp.where(ok, ib, n_blocks - 1).astype(jnp.int32)
    lo = jnp.where(ok, lo, 0).astype(jnp.int32)
    hi = jnp.where(ok, hi, 0).astype(jnp.int32)
    return start, ib, ie, lo, hi, n_items.astype(jnp.int32).reshape(1)


def kernel(x, positions, w_in, b_gate, q_norm_g, kv_norm_g, w_uq, w_uk, w_uv, swa_sinks, rel_table,
           w_br_mla, w_br_swa, w_out, ln1_g, ln1_b, w_router, router_bias, w_gate_up, w_down,
           w_shared_gate_up, w_shared_down, ln2_g, ln2_b):
    B, S, D = x.shape
    T = B * S
    H = MLA_HEADS
    E = N_EXPERTS
    assert w_in.shape[0] == DEPTH == 1
    qr, kvr = w_uq.shape[1], w_uk.shape[1]
    rd = MLA_ROPE_DIM
    qw = SWA_Q_HEADS * SWA_HEAD_DIM
    kw = SWA_KV_HEADS * SWA_HEAD_DIM

    sizes = (qr, kvr, rd, qw, kw, kw, N_BRANCHES * D)
    bounds = np.cumsum(sizes)[:-1].tolist()
    w_cq, w_ckv, w_kr, w_qs, w_ks, w_vs, w_gate = jnp.split(w_in[0], bounds, axis=-1)
    zpad = jnp.zeros((D, LANE - rd), F32)
    segs = {"g0": w_gate[:, :D], "g1": w_gate[:, D:], "q_s": w_qs, "c_q": w_cq, "c_kv": w_ckv,
            "k_s": w_ks, "v_s": w_vs,
            "kra": jnp.concatenate([w_kr, zpad], axis=1),
            "krb": jnp.concatenate([_rot_cols(w_kr), zpad], axis=1)}
    lay, total, order = _layout([(n, int(a.shape[1])) for n, a in segs.items()])
    w_p = jnp.concatenate([segs[n] for n in order], axis=1).astype(BF16)

    wq = w_uq[0].reshape(qr, H, MLA_NOPE_DIM + rd)
    wq_nope, wq_rope = wq[..., :MLA_NOPE_DIM], wq[..., MLA_NOPE_DIM:]
    z64 = jnp.zeros((qr, H, LANE - rd), F32)
    wqa = jnp.concatenate([wq_nope, wq_rope, z64], axis=-1).reshape(qr, H * 2 * LANE).astype(BF16)
    wqb = jnp.concatenate([_rot_cols(wq_rope), z64], axis=-1).reshape(qr, H * LANE).astype(BF16)
    wuk = w_uk[0].astype(BF16)
    wuv = w_uv[0].astype(BF16)

    inv = ROPE_THETA ** (-jnp.arange(0, rd, 2, dtype=F32) / rd)
    ang = positions.astype(F32).reshape(T, 1) * inv
    zc = jnp.zeros((T, LANE - rd), F32)
    cosp = jnp.concatenate([jnp.cos(ang), jnp.cos(ang), zc], axis=1)
    sinp = jnp.concatenate([jnp.sin(ang), jnp.sin(ang), zc], axis=1)

    qi = jnp.arange(BLOCK)[:, None]
    kj = jnp.arange(2 * BLOCK)[None, :]
    dist = qi + BLOCK - kj
    bias = rel_table[_t5_bucket(dist)].astype(F32).transpose(2, 0, 1)
    in_window = (dist >= 0) & (dist < WINDOW)
    bias_masked = jnp.where(in_window[None], bias, NEG_BIG)

    per = E // N_GROUPS
    perm = np.arange(E).reshape(N_GROUPS, per).T.reshape(-1)
    wrt = w_router[0].T[perm].astype(BF16)
    bias_col = router_bias[0].astype(F32)[perm].reshape(E, 1)
    rt = _tile(T, ROUTER_TM)
    utri = jnp.asarray(np.triu(np.ones((rt, rt), np.float32), k=1), dtype=BF16)

    x2 = x.reshape(T, D)

    proj = _in_proj(x2, w_p)
    q, k, v = _mla_proj(proj, lay, cosp, sinp, q_norm_g[0].reshape(1, qr), kv_norm_g[0].reshape(1, kvr),
                        wqa, wqb, wuk, wuv, H)
    o_m = _mla_attn(q.reshape(B, S, -1), k.reshape(B, S, -1), v.reshape(B, S, -1), H).reshape(T, -1)
    o_s = _swa_attn(proj.reshape(B, S, total), lay, swa_sinks[0].astype(F32), bias_masked).reshape(T, qw)
    merged = _merge(o_m, o_s, proj, lay, b_gate[0], w_br_mla[0].astype(BF16), w_br_swa[0].astype(BF16))
    h, hb, hp, scores_t = _out_ln(merged, x2, w_out[0].astype(BF16), ln1_g[0].reshape(1, D),
                                  ln1_b[0].reshape(1, D), wrt)

    te_t, w_t, rk_t, cnt = _router(scores_t, bias_col, utri)
    inv_perm = np.argsort(perm)
    gs = cnt[:, 0].astype(jnp.int32)[inv_perm]
    tm = MOE_TM
    N = T * TOP_K
    assert N % tm == 0
    start, ib, ie, lo, hi, n_items = _work_items(gs, tm, N // tm)
    dest_t = rk_t + jnp.sum(jnp.where(te_t[:, :, None] == jnp.arange(E, dtype=jnp.int32),
                                      start.astype(jnp.int32), 0), axis=-1)

    xs = _dispatch(dest_t, hp)
    ys = _moe_experts(ib, ie, lo, hi, n_items, xs, w_gate_up[0].astype(BF16), w_down[0].astype(BF16), tm)
    out = _moe_combine(dest_t, ys, w_t.T, h, hb, w_shared_gate_up[0].astype(BF16),
                       w_shared_down[0].astype(BF16), ln2_g[0].reshape(1, D), ln2_b[0].reshape(1, D))
    return out.reshape(B, S, D)
```

```python
import functools
import math

import jax
import jax.numpy as jnp
import numpy as np
from jax import lax
from jax.experimental import pallas as pl
from jax.experimental.pallas import tpu as pltpu

F32 = jnp.float32
BF16 = jnp.bfloat16

MLA_HEADS = 8
MLA_NOPE_DIM = 128
MLA_ROPE_DIM = 64
MLA_V_DIM = 128
ROPE_THETA = 10000.0
SWA_Q_HEADS = 16
SWA_KV_HEADS = 4
SWA_HEAD_DIM = 64
WINDOW = 128
BLOCK = 128
REL_BUCKETS = 32
REL_MAX_DIST = 128
N_BRANCHES = 2
N_EXPERTS = 64
TOP_K = 8
N_GROUPS = 8
TOPK_GROUPS = 4
ROUTED_SCALE = 2.5
DEPTH = 1
ALPHA = (2 * DEPTH) ** 0.25
LN_EPS = 1e-5
RMS_EPS = 1e-6

LANE = 128
NEG_BIG = -1e30

PROJ_TM = 512
MLA_PROJ_TM = 512
MLA_TQ = 512
MERGE_TM = 512
OUT_TM = 512
MOE_TM = 512
COMBINE_TM = 256
ROUTER_TM = 512
DISPATCH_TM = 256
VMEM_LIMIT = 56 * 1024 * 1024


def _cparams(sem):
    return pltpu.CompilerParams(dimension_semantics=sem, vmem_limit_bytes=VMEM_LIMIT)


def _tile(n, t):
    t = min(n, t)
    assert n % t == 0, (n, t)
    return t


def _in_proj_kernel(x_ref, w_ref, o_ref):
    x = x_ref[...].astype(BF16)
    o_ref[...] = jnp.dot(x, w_ref[...], preferred_element_type=F32).astype(o_ref.dtype)


def _in_proj(x2, w_p):
    T, D = x2.shape
    N = w_p.shape[1]
    tn = N // 2 if (N // 2) % LANE == 0 and N % 2 == 0 else N
    tm = _tile(T, PROJ_TM)
    return pl.pallas_call(
        _in_proj_kernel,
        out_shape=jax.ShapeDtypeStruct((T, N), BF16),
        grid=(N // tn, T // tm),
        in_specs=[pl.BlockSpec((tm, D), lambda j, i: (i, 0)),
                  pl.BlockSpec((D, tn), lambda j, i: (0, j))],
        out_specs=pl.BlockSpec((tm, tn), lambda j, i: (i, j)),
        compiler_params=_cparams(("arbitrary", "arbitrary")),
        name="in_proj",
    )(x2, w_p)


def _mla_proj_kernel(cq_ref, ckv_ref, kra_ref, krb_ref, cos_ref, sin_ref, qg_ref, kvg_ref,
                     wqa_ref, wqb_ref, wuk_ref, wuv_ref, q_ref, k_ref, v_ref, *, heads, scale):
    cos = cos_ref[...]
    sin = sin_ref[...]
    cq = cq_ref[...].astype(F32)
    qn = cq * lax.rsqrt(jnp.mean(cq * cq, axis=-1, keepdims=True) + RMS_EPS) * qg_ref[...]
    qn = qn.astype(BF16)
    qa = jnp.dot(qn, wqa_ref[...], preferred_element_type=F32)
    qb = jnp.dot(qn, wqb_ref[...], preferred_element_type=F32)
    ckv = ckv_ref[...].astype(F32)
    cn = ckv * lax.rsqrt(jnp.mean(ckv * ckv, axis=-1, keepdims=True) + RMS_EPS) * kvg_ref[...]
    cn = cn.astype(BF16)
    kn = jnp.dot(cn, wuk_ref[...], preferred_element_type=F32)
    v_ref[...] = jnp.dot(cn, wuv_ref[...], preferred_element_type=F32).astype(v_ref.dtype)
    krope = (kra_ref[...].astype(F32) * cos + krb_ref[...].astype(F32) * sin).astype(k_ref.dtype)
    for h in range(heads):
        lo = h * 2 * LANE
        q_ref[:, lo:lo + LANE] = (qa[:, lo:lo + LANE] * scale).astype(q_ref.dtype)
        q_ref[:, lo + LANE:lo + 2 * LANE] = (
            (qa[:, lo + LANE:lo + 2 * LANE] * cos + qb[:, h * LANE:(h + 1) * LANE] * sin) * scale
        ).astype(q_ref.dtype)
        k_ref[:, lo:lo + LANE] = kn[:, h * LANE:(h + 1) * LANE].astype(k_ref.dtype)
        k_ref[:, lo + LANE:lo + 2 * LANE] = krope


def _mla_proj(proj, lay, cosp, sinp, qg, kvg, wqa, wqb, wuk, wuv, heads):
    T = proj.shape[0]
    tm = _tile(T, MLA_PROJ_TM)
    qr, kvr = wqa.shape[0], wuk.shape[0]

    def col(name, width):
        off = lay[name]
        assert off % width == 0
        return pl.BlockSpec((tm, width), lambda i, o=off // width: (i, o))

    row = lambda w: pl.BlockSpec((tm, w), lambda i: (i, 0))
    full = lambda a: pl.BlockSpec(a.shape, lambda i: (0,) * a.ndim)
    scale = (MLA_NOPE_DIM + MLA_ROPE_DIM) ** -0.5
    return pl.pallas_call(
        functools.partial(_mla_proj_kernel, heads=heads, scale=scale),
        out_shape=(jax.ShapeDtypeStruct((T, heads * 2 * LANE), BF16),
                   jax.ShapeDtypeStruct((T, heads * 2 * LANE), BF16),
                   jax.ShapeDtypeStruct((T, heads * MLA_V_DIM), BF16)),
        grid=(T // tm,),
        in_specs=[col("c_q", qr), col("c_kv", kvr), col("kra", LANE), col("krb", LANE),
                  row(LANE), row(LANE), full(qg), full(kvg), full(wqa), full(wqb), full(wuk), full(wuv)],
        out_specs=(row(heads * 2 * LANE), row(heads * 2 * LANE), row(heads * MLA_V_DIM)),
        compiler_params=_cparams(("arbitrary",)),
        name="mla_proj",
    )(proj, proj, proj, proj, cosp, sinp, qg, kvg, wqa, wqb, wuk, wuv)


def _mla_attn_kernel(q_ref, k_ref, v_ref, o_ref, *, tq):
    i = pl.program_id(2)
    q = q_ref[0]
    dn = (((1,), (1,)), ((), ()))

    def step(kb, vb, carry, mask):
        m, l, acc = carry
        s = lax.dot_general(q, kb, dn, preferred_element_type=F32)
        if mask is not None:
            s = jnp.where(mask, s, NEG_BIG)
        m_new = jnp.maximum(m, jnp.max(s, axis=-1, keepdims=True))
        p = jnp.exp(s - m_new)
        a = jnp.exp(m - m_new)
        l = a * l + jnp.sum(p, axis=-1, keepdims=True)
        acc = a * acc + jnp.dot(p.astype(vb.dtype), vb, preferred_element_type=F32)
        return m_new, l, acc

    def body(j, carry):
        off = pl.multiple_of(j * tq, tq)
        return step(k_ref[0, pl.ds(off, tq), :], v_ref[0, pl.ds(off, tq), :], carry, None)

    init = (jnp.full((tq, 1), NEG_BIG, F32), jnp.zeros((tq, 1), F32),
            jnp.zeros((tq, v_ref.shape[-1]), F32))
    carry = lax.fori_loop(0, i, body, init)
    off = pl.multiple_of(i * tq, tq)
    r = lax.broadcasted_iota(jnp.int32, (tq, tq), 0)
    c = lax.broadcasted_iota(jnp.int32, (tq, tq), 1)
    m, l, acc = step(k_ref[0, pl.ds(off, tq), :], v_ref[0, pl.ds(off, tq), :], carry, c <= r)
    o_ref[0] = (acc / l).astype(o_ref.dtype)


def _mla_attn(q, k, v, heads):
    B, S, _ = q.shape
    tq = _tile(S, MLA_TQ)
    dv = v.shape[-1] // heads
    return pl.pallas_call(
        functools.partial(_mla_attn_kernel, tq=tq),
        out_shape=jax.ShapeDtypeStruct((B, S, heads * dv), BF16),
        grid=(B, heads, S // tq),
        in_specs=[pl.BlockSpec((1, tq, 2 * LANE), lambda b, h, i: (b, i, h)),
                  pl.BlockSpec((1, S, 2 * LANE), lambda b, h, i: (b, 0, h)),
                  pl.BlockSpec((1, S, dv), lambda b, h, i: (b, 0, h))],
        out_specs=pl.BlockSpec((1, tq, dv), lambda b, h, i: (b, i, h)),
        compiler_params=_cparams(("arbitrary", "arbitrary", "arbitrary")),
        name="mla_attn",
    )(q, k, v)


def _swa_kernel(sink_ref, q_ref, kp_ref, kc_ref, vp_ref, vc_ref, bias_ref, o_ref, *, groups, per):
    hd = SWA_HEAD_DIM
    gw = per * hd
    kband = jnp.concatenate([kp_ref[0], kc_ref[0]], axis=0)
    vband = jnp.concatenate([vp_ref[0], vc_ref[0]], axis=0)
    lane = lax.broadcasted_iota(jnp.int32, (BLOCK, gw), 1)
    sels = [jnp.logical_and(lane >= hh * hd, lane < (hh + 1) * hd) for hh in range(per)]
    rowh = lax.broadcasted_iota(jnp.int32, (per * BLOCK, 1), 0) // BLOCK
    dn = (((1,), (1,)), ((), ()))
    for g in range(groups):
        kg = kband[:, g * hd:(g + 1) * hd]
        vg = vband[:, g * hd:(g + 1) * hd]
        krep = jnp.concatenate([kg] * per, axis=1)
        vrep = jnp.concatenate([vg] * per, axis=1)
        qg = q_ref[0, :, g * gw:(g + 1) * gw]
        qs = jnp.concatenate([jnp.where(sels[hh], qg, jnp.zeros_like(qg)) for hh in range(per)], axis=0)
        s = lax.dot_general(qs, krep, dn, preferred_element_type=F32)
        s = s + bias_ref[0, g * per:(g + 1) * per].reshape(per * BLOCK, 2 * BLOCK)
        sink = jnp.full((per * BLOCK, 1), sink_ref[g * per], F32)
        for hh in range(1, per):
            sink = jnp.where(rowh == hh, sink_ref[g * per + hh], sink)
        m = jnp.maximum(jnp.max(s, axis=-1, keepdims=True), sink)
        p = jnp.exp(s - m)
        den = jnp.sum(p, axis=-1, keepdims=True) + jnp.exp(sink - m)
        o = jnp.dot(p.astype(vrep.dtype), vrep, preferred_element_type=F32) * (1.0 / den)
        og = o[0:BLOCK]
        for hh in range(1, per):
            og = jnp.where(sels[hh], o[hh * BLOCK:(hh + 1) * BLOCK], og)
        o_ref[0, :, g * gw:(g + 1) * gw] = og.astype(o_ref.dtype)


def _swa_attn(proj3, lay, sinks, bias_masked):
    B, S, _ = proj3.shape
    nb = S // BLOCK
    groups, per = SWA_KV_HEADS, SWA_Q_HEADS // SWA_KV_HEADS
    qw = SWA_Q_HEADS * SWA_HEAD_DIM
    kw = SWA_KV_HEADS * SWA_HEAD_DIM
    assert lay["q_s"] % qw == 0 and lay["k_s"] % kw == 0 and lay["v_s"] % kw == 0
    qo, ko, vo = lay["q_s"] // qw, lay["k_s"] // kw, lay["v_s"] // kw
    prev = lambda o: (lambda b, i, s: (b, jnp.maximum(i - 1, 0), o))
    cur = lambda o: (lambda b, i, s: (b, i, o))
    grid_spec = pltpu.PrefetchScalarGridSpec(
        num_scalar_prefetch=1,
        grid=(B, nb),
        in_specs=[pl.BlockSpec((1, BLOCK, qw), cur(qo)),
                  pl.BlockSpec((1, BLOCK, kw), prev(ko)),
                  pl.BlockSpec((1, BLOCK, kw), cur(ko)),
                  pl.BlockSpec((1, BLOCK, kw), prev(vo)),
                  pl.BlockSpec((1, BLOCK, kw), cur(vo)),
                  pl.BlockSpec((1,) + bias_masked.shape[1:], lambda b, i, s: (jnp.minimum(i, 1), 0, 0, 0))],
        out_specs=pl.BlockSpec((1, BLOCK, qw), lambda b, i, s: (b, i, 0)),
    )
    return pl.pallas_call(
        functools.partial(_swa_kernel, groups=groups, per=per),
        out_shape=jax.ShapeDtypeStruct((B, S, qw), BF16),
        grid_spec=grid_spec,
        compiler_params=_cparams(("arbitrary", "arbitrary")),
        name="swa_attn",
    )(sinks, proj3, proj3, proj3, proj3, proj3, bias_masked)


def _merge_kernel(om_ref, os_ref, g0_ref, g1_ref, bg_ref, wm_ref, ws_ref, o_ref):
    ym = jnp.dot(om_ref[...], wm_ref[...], preferred_element_type=F32)
    ys = jnp.dot(os_ref[...], ws_ref[...], preferred_element_type=F32)
    g0 = jax.nn.sigmoid(g0_ref[...].astype(F32) + bg_ref[0:1, :])
    g1 = jax.nn.sigmoid(g1_ref[...].astype(F32) + bg_ref[1:2, :])
    o_ref[...] = (g0 * ym + g1 * ys).astype(o_ref.dtype)


def _merge(om, osw, proj, lay, bg, wm, ws):
    T = om.shape[0]
    D = wm.shape[1]
    tm = _tile(T, MERGE_TM)
    assert lay["g0"] % D == 0 and lay["g1"] % D == 0
    row = lambda w: pl.BlockSpec((tm, w), lambda i: (i, 0))
    full = lambda a: pl.BlockSpec(a.shape, lambda i: (0,) * a.ndim)
    return pl.pallas_call(
        _merge_kernel,
        out_shape=jax.ShapeDtypeStruct((T, D), BF16),
        grid=(T // tm,),
        in_specs=[row(om.shape[1]), row(osw.shape[1]),
                  pl.BlockSpec((tm, D), lambda i, o=lay["g0"] // D: (i, o)),
                  pl.BlockSpec((tm, D), lambda i, o=lay["g1"] // D: (i, o)),
                  full(bg), full(wm), full(ws)],
        out_specs=row(D),
        compiler_params=_cparams(("arbitrary",)),
        name="merge",
    )(om, osw, proj, proj, bg, wm, ws)


def _layer_norm(z, g, b):
    mu = jnp.mean(z, axis=-1, keepdims=True)
    zc = z - mu
    var = jnp.mean(zc * zc, axis=-1, keepdims=True)
    return zc * lax.rsqrt(var + LN_EPS) * g + b


def _pack_pairs(y):
    n = y.shape[1] // 2
    lo = pltpu.bitcast(y[:, :n].astype(BF16).astype(F32), jnp.uint32)
    hi = pltpu.bitcast(y[:, n:].astype(BF16).astype(F32), jnp.uint32)
    return (lo >> 16) | (hi & jnp.uint32(0xFFFF0000))


def _unpack_pairs(w):
    lo = pltpu.bitcast(w << 16, F32)
    hi = pltpu.bitcast(w & jnp.uint32(0xFFFF0000), F32)
    return lo, hi


def _out_ln_kernel(mg_ref, x_ref, wo_ref, g_ref, b_ref, wrt_ref, h_ref, hb_ref, hp_ref, st_ref):
    mix = jnp.dot(mg_ref[...], wo_ref[...], preferred_element_type=F32)
    h = _layer_norm(ALPHA * x_ref[...] + mix, g_ref[...], b_ref[...])
    h_ref[...] = h
    hb = h.astype(BF16)
    hb_ref[...] = hb
    hp_ref[...] = _pack_pairs(h)
    logits_t = lax.dot_general(wrt_ref[...], hb, (((1,), (1,)), ((), ())), preferred_element_type=F32)
    st_ref[...] = jax.nn.sigmoid(logits_t)


def _out_ln(merged, x2, wo, g, b, wrt):
    T, D = x2.shape
    E = wrt.shape[0]
    tm = _tile(T, OUT_TM)
    row = lambda w: pl.BlockSpec((tm, w), lambda i: (i, 0))
    full = lambda a: pl.BlockSpec(a.shape, lambda i: (0,) * a.ndim)
    return pl.pallas_call(
        _out_ln_kernel,
        out_shape=(jax.ShapeDtypeStruct((T, D), F32), jax.ShapeDtypeStruct((T, D), BF16),
                   jax.ShapeDtypeStruct((T, D // 2), jnp.uint32), jax.ShapeDtypeStruct((E, T), F32)),
        grid=(T // tm,),
        in_specs=[row(D), row(D), full(wo), full(g), full(b), full(wrt)],
        out_specs=(row(D), row(D), row(D // 2), pl.BlockSpec((E, tm), lambda i: (0, i))),
        compiler_params=_cparams(("arbitrary",)),
        name="out_ln",
    )(merged, x2, wo, g, b, wrt)


def _router_kernel(s_ref, b_ref, u_ref, te_ref, w_ref, rk_ref, cnt_ref, carry_ref, *, per):
    G = N_GROUPS
    tm = s_ref.shape[1]
    ninf = -jnp.inf

    @pl.when(pl.program_id(0) == 0)
    def _():
        carry_ref[...] = jnp.zeros_like(carry_ref)

    giota = lax.broadcasted_iota(jnp.int32, (G, tm), 0)
    s = [s_ref[j * G:(j + 1) * G, :] for j in range(per)]
    sel = [s[j] + b_ref[j * G:(j + 1) * G, :] for j in range(per)]
    eidx = [giota * per + j for j in range(per)]

    m1 = functools.reduce(jnp.maximum, sel)
    j1 = functools.reduce(jnp.minimum, [jnp.where(sel[j] == m1, j, per) for j in range(per)])
    m2 = functools.reduce(jnp.maximum, [jnp.where(j1 == j, ninf, sel[j]) for j in range(per)])
    grp = m1 + m2
    gmask = jnp.zeros((G, tm), jnp.bool_)
    for _ in range(TOPK_GROUPS):
        mx = jnp.max(grp, axis=0, keepdims=True)
        gi = jnp.min(jnp.where(grp == mx, giota, G), axis=0, keepdims=True)
        chosen = giota == gi
        gmask = jnp.logical_or(gmask, chosen)
        grp = jnp.where(chosen, ninf, grp)

    cur = [jnp.where(gmask, sel[j], ninf) for j in range(per)]
    mem = [jnp.zeros((G, tm), F32) for _ in range(per)]
    e_rows, w_rows = [], []
    for _ in range(TOP_K):
        mx = jnp.max(functools.reduce(jnp.maximum, cur), axis=0, keepdims=True)
        cand = functools.reduce(jnp.minimum, [jnp.where(cur[j] == mx, eidx[j], N_EXPERTS) for j in range(per)])
        emin = jnp.min(cand, axis=0, keepdims=True)
        wk = jnp.zeros((1, tm), F32)
        for j in range(per):
            hit = eidx[j] == emin
            wk = wk + jnp.sum(jnp.where(hit, s[j], 0.0), axis=0, keepdims=True)
            mem[j] = jnp.where(hit, 1.0, mem[j])
            cur[j] = jnp.where(hit, ninf, cur[j])
        e_rows.append(emin)
        w_rows.append(wk)
    wsum = functools.reduce(lambda a, b: a + b, w_rows)

    memall = jnp.concatenate(mem, axis=0)
    pre = jnp.dot(memall.astype(BF16), u_ref[...], preferred_element_type=F32)
    rank = pre + carry_ref[:, 0:1]
    tot = carry_ref[...] + jnp.sum(memall, axis=1, keepdims=True)
    carry_ref[...] = tot
    cnt_ref[...] = tot

    kiota = lax.broadcasted_iota(jnp.int32, (TOP_K, tm), 0)
    te = jnp.zeros((TOP_K, tm), jnp.int32)
    wt = jnp.zeros((TOP_K, tm), F32)
    rk = jnp.zeros((TOP_K, tm), F32)
    for k in range(TOP_K):
        rk_k = jnp.zeros((1, tm), F32)
        for j in range(per):
            rk_k = rk_k + jnp.sum(jnp.where(eidx[j] == e_rows[k], rank[j * G:(j + 1) * G, :], 0.0),
                                  axis=0, keepdims=True)
        te = jnp.where(kiota == k, e_rows[k], te)
        wt = jnp.where(kiota == k, w_rows[k] / wsum * ROUTED_SCALE, wt)
        rk = jnp.where(kiota == k, rk_k, rk)
    te_ref[...] = te
    w_ref[...] = wt
    rk_ref[...] = rk.astype(jnp.int32)


def _router(scores_t, bias_col, utri):
    E, T = scores_t.shape
    tm = utri.shape[0]
    per = E // N_GROUPS
    tk = lambda: pl.BlockSpec((TOP_K, tm), lambda i: (0, i))
    return pl.pallas_call(
        functools.partial(_router_kernel, per=per),
        out_shape=(jax.ShapeDtypeStruct((TOP_K, T), jnp.int32), jax.ShapeDtypeStruct((TOP_K, T), F32),
                   jax.ShapeDtypeStruct((TOP_K, T), jnp.int32), jax.ShapeDtypeStruct((E, LANE), F32)),
        grid=(T // tm,),
        in_specs=[pl.BlockSpec((E, tm), lambda i: (0, i)),
                  pl.BlockSpec((E, 1), lambda i: (0, 0)),
                  pl.BlockSpec((tm, tm), lambda i: (0, 0))],
        out_specs=(tk(), tk(), tk(), pl.BlockSpec((E, LANE), lambda i: (0, 0))),
        scratch_shapes=[pltpu.VMEM((E, LANE), F32)],
        compiler_params=_cparams(("arbitrary",)),
        name="router",
    )(scores_t, bias_col, utri)


def _dispatch_kernel(dest_ref, h_ref, xs_ref, sem):
    tm = h_ref.shape[0]

    def body(r, c):
        for k in range(TOP_K):
            pltpu.make_async_copy(h_ref.at[pl.ds(r, 1)], xs_ref.at[pl.ds(dest_ref[k, r], 1)],
                                  sem).start(priority=k % 2)
        return c

    lax.fori_loop(0, tm, body, 0)
    for k in range(TOP_K):
        pltpu.make_async_copy(h_ref, xs_ref.at[pl.ds(0, tm)], sem).wait()


def _dispatch(dest_t, hp):
    T, W = hp.shape
    tm = _tile(T, DISPATCH_TM)
    return pl.pallas_call(
        _dispatch_kernel,
        out_shape=jax.ShapeDtypeStruct((T * TOP_K, W), hp.dtype),
        grid=(T // tm,),
        in_specs=[pl.BlockSpec((TOP_K, tm), lambda i: (0, i), memory_space=pltpu.SMEM),
                  pl.BlockSpec((tm, W), lambda i: (i, 0))],
        out_specs=pl.BlockSpec(memory_space=pl.ANY),
        scratch_shapes=[pltpu.SemaphoreType.DMA(())],
        compiler_params=_cparams(("arbitrary",)),
        name="dispatch",
    )(dest_t, hp)


def _moe_kernel(ib_ref, ie_ref, lo_ref, hi_ref, n_ref, x_ref, wgu_ref, wd_ref, y_ref, acc_ref, *, ff):
    i = pl.program_id(0)
    n = n_ref[0]
    last_i = ib_ref.shape[0] - 1
    tm = x_ref.shape[0]
    half = x_ref.shape[1]

    @pl.when(i < n)
    def _():
        b = ib_ref[i]
        first = jnp.logical_or(i == 0, ib_ref[jnp.maximum(i - 1, 0)] != b)
        last = jnp.logical_or(i == n - 1, ib_ref[jnp.minimum(i + 1, last_i)] != b)
        xlo, xhi = _unpack_pairs(x_ref[...])
        h = (jnp.dot(xlo.astype(BF16), wgu_ref[0, :half, :], preferred_element_type=F32)
             + jnp.dot(xhi.astype(BF16), wgu_ref[0, half:, :], preferred_element_type=F32))
        a = (jax.nn.silu(h[:, :ff]) * h[:, ff:]).astype(BF16)
        y = jnp.dot(a, wd_ref[0], preferred_element_type=F32)
        rows = lax.broadcasted_iota(jnp.int32, (tm, 1), 0)
        keep = jnp.logical_and(rows >= lo_ref[i], rows < hi_ref[i])
        y = jnp.where(keep, y, 0.0)

        @pl.when(first)
        def _():
            acc_ref[...] = y

        @pl.when(jnp.logical_not(first))
        def _():
            acc_ref[...] += y

        @pl.when(last)
        def _():
            y_ref[...] = _pack_pairs(acc_ref[...])


def _moe_experts(ib, ie, lo, hi, n_items, xs, wgu, wd, tm):
    N, W = xs.shape
    ff = wd.shape[1]
    D = wd.shape[2]
    grid_spec = pltpu.PrefetchScalarGridSpec(
        num_scalar_prefetch=5,
        grid=(ib.shape[0],),
        in_specs=[pl.BlockSpec((tm, W), lambda i, ib, ie, lo, hi, n: (ib[i], 0)),
                  pl.BlockSpec((1, D, 2 * ff), lambda i, ib, ie, lo, hi, n: (ie[i], 0, 0)),
                  pl.BlockSpec((1, ff, D), lambda i, ib, ie, lo, hi, n: (ie[i], 0, 0))],
        out_specs=pl.BlockSpec((tm, W), lambda i, ib, ie, lo, hi, n: (ib[i], 0)),
        scratch_shapes=[pltpu.VMEM((tm, D), F32)],
    )
    return pl.pallas_call(
        functools.partial(_moe_kernel, ff=ff),
        out_shape=jax.ShapeDtypeStruct((N, W), jnp.uint32),
        grid_spec=grid_spec,
        compiler_params=_cparams(("arbitrary",)),
        name="moe_experts",
    )(ib, ie, lo, hi, n_items, xs, wgu, wd)


def _combine_kernel(dest_ref, ys_ref, w_ref, h_ref, hb_ref, wsg_ref, wsd_ref, g_ref, b_ref, o_ref,
                    ybuf, sem, *, ff):
    tm = h_ref.shape[0]

    def body(r, c):
        for k in range(TOP_K):
            pltpu.make_async_copy(ys_ref.at[pl.ds(dest_ref[k, r], 1)], ybuf.at[k, pl.ds(r, 1)],
                                  sem).start(priority=k % 2)
        return c

    lax.fori_loop(0, tm, body, 0)
    s = jnp.dot(hb_ref[...], wsg_ref[...], preferred_element_type=F32)
    a = (jax.nn.silu(s[:, :ff]) * s[:, ff:]).astype(BF16)
    shared = jnp.dot(a, wsd_ref[...], preferred_element_type=F32)
    for k in range(TOP_K):
        pltpu.make_async_copy(ys_ref.at[pl.ds(0, tm)], ybuf.at[k], sem).wait()
    w = w_ref[...]
    rlo = jnp.zeros((tm, ybuf.shape[2]), F32)
    rhi = jnp.zeros((tm, ybuf.shape[2]), F32)
    for k in range(TOP_K):
        lo, hi = _unpack_pairs(ybuf[k])
        rlo = rlo + lo * w[:, k:k + 1]
        rhi = rhi + hi * w[:, k:k + 1]
    routed = jnp.concatenate([rlo, rhi], axis=1)
    o_ref[...] = _layer_norm(ALPHA * h_ref[...] + routed + shared, g_ref[...], b_ref[...])


def _moe_combine(dest_t, ys, w, h, hb, wsg, wsd, g, b):
    T, D = h.shape
    ff = wsd.shape[0]
    tm = _tile(T, COMBINE_TM)
    row = lambda wd_: pl.BlockSpec((tm, wd_), lambda i: (i, 0))
    full = lambda a: pl.BlockSpec(a.shape, lambda i: (0,) * a.ndim)
    return pl.pallas_call(
        functools.partial(_combine_kernel, ff=ff),
        out_shape=jax.ShapeDtypeStruct((T, D), F32),
        grid=(T // tm,),
        in_specs=[pl.BlockSpec((TOP_K, tm), lambda i: (0, i), memory_space=pltpu.SMEM),
                  pl.BlockSpec(memory_space=pl.ANY), row(TOP_K), row(D), row(D),
                  full(wsg), full(wsd), full(g), full(b)],
        out_specs=row(D),
        scratch_shapes=[pltpu.VMEM((TOP_K, tm, ys.shape[1]), ys.dtype), pltpu.SemaphoreType.DMA(())],
        compiler_params=_cparams(("arbitrary",)),
        name="moe_combine",
    )(dest_t, ys, w, h, hb, wsg, wsd, g, b)


def _rot_cols(w):
    half = w.shape[-1] // 2
    return jnp.concatenate([-w[..., half:], w[..., :half]], axis=-1)


def _t5_bucket(dist):
    n = jnp.maximum(dist, 0)
    max_exact = REL_BUCKETS // 2
    large = max_exact + (jnp.log(jnp.maximum(n, 1).astype(F32) / max_exact)
                         / math.log(REL_MAX_DIST / max_exact) * (REL_BUCKETS - max_exact)).astype(jnp.int32)
    large = jnp.minimum(large, REL_BUCKETS - 1)
    return jnp.where(n < max_exact, n, large)


def _layout(widths):
    order = sorted(range(len(widths)), key=lambda j: -widths[j][1])
    lay, off = {}, 0
    for j in order:
        name, w = widths[j]
        assert off % w == 0, (name, off, w)
        lay[name] = off
        off += w
    return lay, off, [widths[j][0] for j in order]


def _work_items(gs, tm, n_blocks):
    E = gs.shape[0]
    end = jnp.cumsum(gs)
    start = end - gs
    b0 = start // tm
    n_e = jnp.where(gs > 0, (end + tm - 1) // tm - b0, 0)
    icum = jnp.cumsum(n_e)
    n_items = icum[-1]
    cap = n_blocks + E
    idx = jnp.arange(cap, dtype=jnp.int32)
    ie = jnp.minimum(jnp.sum(idx[:, None] >= icum[None, :], axis=1), E - 1).astype(jnp.int32)
    ib = (b0[ie] + idx - (icum - n_e)[ie]).astype(jnp.int32)
    lo = jnp.maximum(start[ie] - ib * tm, 0)
    hi = jnp.minimum(end[ie] - ib * tm, tm)
    ok = idx < n_items
    ib = jnp.where(ok, ib, n_blocks - 1).astype(jnp.int32)
    lo = jnp.where(ok, lo, 0).astype(jnp.int32)
    hi = jnp.where(ok, hi, 0).astype(jnp.int32)
    return start, ib, ie, lo, hi, n_items.astype(jnp.int32).reshape(1)


def kernel(x, positions, w_in, b_gate, q_norm_g, kv_norm_g, w_uq, w_uk, w_uv, swa_sinks, rel_table,
           w_br_mla, w_br_swa, w_out, ln1_g, ln1_b, w_router, router_bias, w_gate_up, w_down,
           w_shared_gate_up, w_shared_down, ln2_g, ln2_b):
    B, S, D = x.shape
    T = B * S
    H = MLA_HEADS
    E = N_EXPERTS
    assert w_in.shape[0] == DEPTH == 1
    qr, kvr = w_uq.shape[1], w_uk.shape[1]
    rd = MLA_ROPE_DIM
    qw = SWA_Q_HEADS * SWA_HEAD_DIM
    kw = SWA_KV_HEADS * SWA_HEAD_DIM

    sizes = (qr, kvr, rd, qw, kw, kw, N_BRANCHES * D)
    bounds = np.cumsum(sizes)[:-1].tolist()
    w_cq, w_ckv, w_kr, w_qs, w_ks, w_vs, w_gate = jnp.split(w_in[0], bounds, axis=-1)
    zpad = jnp.zeros((D, LANE - rd), F32)
    segs = {"g0": w_gate[:, :D], "g1": w_gate[:, D:], "q_s": w_qs * (SWA_HEAD_DIM ** -0.5), "c_q": w_cq, "c_kv": w_ckv,
            "k_s": w_ks, "v_s": w_vs,
            "kra": jnp.concatenate([w_kr, zpad], axis=1),
            "krb": jnp.concatenate([_rot_cols(w_kr), zpad], axis=1)}
    lay, total, order = _layout([(n, int(a.shape[1])) for n, a in segs.items()])
    w_p = jnp.concatenate([segs[n] for n in order], axis=1).astype(BF16)

    wq = w_uq[0].reshape(qr, H, MLA_NOPE_DIM + rd)
    wq_nope, wq_rope = wq[..., :MLA_NOPE_DIM], wq[..., MLA_NOPE_DIM:]
    z64 = jnp.zeros((qr, H, LANE - rd), F32)
    wqa = jnp.concatenate([wq_nope, wq_rope, z64], axis=-1).reshape(qr, H * 2 * LANE).astype(BF16)
    wqb = jnp.concatenate([_rot_cols(wq_rope), z64], axis=-1).reshape(qr, H * LANE).astype(BF16)
    wuk = w_uk[0].astype(BF16)
    wuv = w_uv[0].astype(BF16)

    inv = ROPE_THETA ** (-jnp.arange(0, rd, 2, dtype=F32) / rd)
    ang = positions.astype(F32).reshape(T, 1) * inv
    zc = jnp.zeros((T, LANE - rd), F32)
    cosp = jnp.concatenate([jnp.cos(ang), jnp.cos(ang), zc], axis=1)
    sinp = jnp.concatenate([jnp.sin(ang), jnp.sin(ang), zc], axis=1)

    qi = jnp.arange(BLOCK)[:, None]
    kj = jnp.arange(2 * BLOCK)[None, :]
    dist = qi + BLOCK - kj
    bias = rel_table[_t5_bucket(dist)].astype(F32).transpose(2, 0, 1)
    in_window = (dist >= 0) & (dist < WINDOW)
    bias_masked = jnp.stack([jnp.where((in_window & (kj >= BLOCK))[None], bias, NEG_BIG),
                             jnp.where(in_window[None], bias, NEG_BIG)])

    per = E // N_GROUPS
    perm = np.arange(E).reshape(N_GROUPS, per).T.reshape(-1)
    wrt = w_router[0].T[perm].astype(BF16)
    bias_col = router_bias[0].astype(F32)[perm].reshape(E, 1)
    rt = _tile(T, ROUTER_TM)
    utri = jnp.asarray(np.triu(np.ones((rt, rt), np.float32), k=1), dtype=BF16)

    x2 = x.reshape(T, D)

    proj = _in_proj(x2, w_p)
    q, k, v = _mla_proj(proj, lay, cosp, sinp, q_norm_g[0].reshape(1, qr), kv_norm_g[0].reshape(1, kvr),
                        wqa, wqb, wuk, wuv, H)
    o_m = _mla_attn(q.reshape(B, S, -1), k.reshape(B, S, -1), v.reshape(B, S, -1), H).reshape(T, -1)
    o_s = _swa_attn(proj.reshape(B, S, total), lay, swa_sinks[0].astype(F32), bias_masked).reshape(T, qw)
    merged = _merge(o_m, o_s, proj, lay, b_gate[0], w_br_mla[0].astype(BF16), w_br_swa[0].astype(BF16))
    h, hb, hp, scores_t = _out_ln(merged, x2, w_out[0].astype(BF16), ln1_g[0].reshape(1, D),
                                  ln1_b[0].reshape(1, D), wrt)

    te_t, w_t, rk_t, cnt = _router(scores_t, bias_col, utri)
    inv_perm = np.argsort(perm)
    gs = cnt[:, 0].astype(jnp.int32)[inv_perm]
    tm = MOE_TM
    N = T * TOP_K
    assert N % tm == 0
    start, ib, ie, lo, hi, n_items = _work_items(gs, tm, N // tm)
    dest_t = rk_t + jnp.sum(jnp.where(te_t[:, :, None] == jnp.arange(E, dtype=jnp.int32),
                                      start.astype(jnp.int32), 0), axis=-1)

    xs = _dispatch(dest_t, hp)
    ys = _moe_experts(ib, ie, lo, hi, n_items, xs, w_gate_up[0].astype(BF16), w_down[0].astype(BF16), tm)
    out = _moe_combine(dest_t, ys, w_t.T, h, hb, w_shared_gate_up[0].astype(BF16),
                       w_shared_down[0].astype(BF16), ln2_g[0].reshape(1, D), ln2_b[0].reshape(1, D))
    return out.reshape(B, S, D)
```

```python
import functools
import math

import jax
import jax.numpy as jnp
import numpy as np
from jax import lax
from jax.experimental import pallas as pl
from jax.experimental.pallas import tpu as pltpu

F32 = jnp.float32
BF16 = jnp.bfloat16

MLA_HEADS = 8
MLA_NOPE_DIM = 128
MLA_ROPE_DIM = 64
MLA_V_DIM = 128
ROPE_THETA = 10000.0
SWA_Q_HEADS = 16
SWA_KV_HEADS = 4
SWA_HEAD_DIM = 64
WINDOW = 128
BLOCK = 128
REL_BUCKETS = 32
REL_MAX_DIST = 128
N_BRANCHES = 2
N_EXPERTS = 64
TOP_K = 8
N_GROUPS = 8
TOPK_GROUPS = 4
ROUTED_SCALE = 2.5
DEPTH = 1
ALPHA = (2 * DEPTH) ** 0.25
LN_EPS = 1e-5
RMS_EPS = 1e-6

LANE = 128
NEG_BIG = -1e30

PROJ_TM = 512
MLA_PROJ_TM = 512
MLA_TQ = 512
MERGE_TM = 512
OUT_TM = 512
MOE_TM = 512
COMBINE_TM = 256
ROUTER_TM = 512
DISPATCH_TM = 256
VMEM_LIMIT = 56 * 1024 * 1024


def _cparams(sem):
    return pltpu.CompilerParams(dimension_semantics=sem, vmem_limit_bytes=VMEM_LIMIT)


def _tile(n, t):
    t = min(n, t)
    assert n % t == 0, (n, t)
    return t


def _in_proj_kernel(x_ref, w_ref, o_ref):
    x = x_ref[...].astype(BF16)
    o_ref[...] = jnp.dot(x, w_ref[...], preferred_element_type=F32).astype(o_ref.dtype)


def _in_proj(x2, w_p):
    T, D = x2.shape
    N = w_p.shape[1]
    tn = N // 2 if (N // 2) % LANE == 0 and N % 2 == 0 else N
    tm = _tile(T, PROJ_TM)
    return pl.pallas_call(
        _in_proj_kernel,
        out_shape=jax.ShapeDtypeStruct((T, N), BF16),
        grid=(N // tn, T // tm),
        in_specs=[pl.BlockSpec((tm, D), lambda j, i: (i, 0)),
                  pl.BlockSpec((D, tn), lambda j, i: (0, j))],
        out_specs=pl.BlockSpec((tm, tn), lambda j, i: (i, j)),
        compiler_params=_cparams(("arbitrary", "arbitrary")),
        name="in_proj",
    )(x2, w_p)


def _mla_proj_kernel(cq_ref, ckv_ref, kra_ref, krb_ref, cos_ref, sin_ref, qg_ref, kvg_ref,
                     wqa_ref, wqb_ref, wuk_ref, wuv_ref, q_ref, k_ref, v_ref, *, heads, scale):
    cos = cos_ref[...]
    sin = sin_ref[...]
    cq = cq_ref[...].astype(F32)
    qn = cq * lax.rsqrt(jnp.mean(cq * cq, axis=-1, keepdims=True) + RMS_EPS) * qg_ref[...]
    qn = qn.astype(BF16)
    qa = jnp.dot(qn, wqa_ref[...], preferred_element_type=F32)
    qb = jnp.dot(qn, wqb_ref[...], preferred_element_type=F32)
    ckv = ckv_ref[...].astype(F32)
    cn = ckv * lax.rsqrt(jnp.mean(ckv * ckv, axis=-1, keepdims=True) + RMS_EPS) * kvg_ref[...]
    cn = cn.astype(BF16)
    kn = jnp.dot(cn, wuk_ref[...], preferred_element_type=F32)
    v_ref[...] = jnp.dot(cn, wuv_ref[...], preferred_element_type=F32).astype(v_ref.dtype)
    krope = (kra_ref[...].astype(F32) * cos + krb_ref[...].astype(F32) * sin).astype(k_ref.dtype)
    for h in range(heads):
        lo = h * 2 * LANE
        q_ref[:, lo:lo + LANE] = (qa[:, lo:lo + LANE] * scale).astype(q_ref.dtype)
        q_ref[:, lo + LANE:lo + 2 * LANE] = (
            (qa[:, lo + LANE:lo + 2 * LANE] * cos + qb[:, h * LANE:(h + 1) * LANE] * sin) * scale
        ).astype(q_ref.dtype)
        k_ref[:, lo:lo + LANE] = kn[:, h * LANE:(h + 1) * LANE].astype(k_ref.dtype)
        k_ref[:, lo + LANE:lo + 2 * LANE] = krope


def _mla_proj(proj, lay, cosp, sinp, qg, kvg, wqa, wqb, wuk, wuv, heads):
    T = proj.shape[0]
    tm = _tile(T, MLA_PROJ_TM)
    qr, kvr = wqa.shape[0], wuk.shape[0]

    def col(name, width):
        off = lay[name]
        assert off % width == 0
        return pl.BlockSpec((tm, width), lambda i, o=off // width: (i, o))

    row = lambda w: pl.BlockSpec((tm, w), lambda i: (i, 0))
    full = lambda a: pl.BlockSpec(a.shape, lambda i: (0,) * a.ndim)
    scale = (MLA_NOPE_DIM + MLA_ROPE_DIM) ** -0.5
    return pl.pallas_call(
        functools.partial(_mla_proj_kernel, heads=heads, scale=scale),
        out_shape=(jax.ShapeDtypeStruct((T, heads * 2 * LANE), BF16),
                   jax.ShapeDtypeStruct((T, heads * 2 * LANE), BF16),
                   jax.ShapeDtypeStruct((T, heads * MLA_V_DIM), BF16)),
        grid=(T // tm,),
        in_specs=[col("c_q", qr), col("c_kv", kvr), col("kra", LANE), col("krb", LANE),
                  row(LANE), row(LANE), full(qg), full(kvg), full(wqa), full(wqb), full(wuk), full(wuv)],
        out_specs=(row(heads * 2 * LANE), row(heads * 2 * LANE), row(heads * MLA_V_DIM)),
        compiler_params=_cparams(("arbitrary",)),
        name="mla_proj",
    )(proj, proj, proj, proj, cosp, sinp, qg, kvg, wqa, wqb, wuk, wuv)


def _mla_attn_kernel(q_ref, k_ref, v_ref, o_ref, *, tq):
    i = pl.program_id(2)
    q = q_ref[0]
    dn = (((1,), (1,)), ((), ()))

    def step(kb, vb, carry, mask):
        m, l, acc = carry
        s = lax.dot_general(q, kb, dn, preferred_element_type=F32)
        if mask is not None:
            s = jnp.where(mask, s, NEG_BIG)
        m_new = jnp.maximum(m, jnp.max(s, axis=-1, keepdims=True))
        p = jnp.exp(s - m_new)
        a = jnp.exp(m - m_new)
        l = a * l + jnp.sum(p, axis=-1, keepdims=True)
        acc = a * acc + jnp.dot(p.astype(vb.dtype), vb, preferred_element_type=F32)
        return m_new, l, acc

    def body(j, carry):
        off = pl.multiple_of(j * tq, tq)
        return step(k_ref[0, pl.ds(off, tq), :], v_ref[0, pl.ds(off, tq), :], carry, None)

    init = (jnp.full((tq, 1), NEG_BIG, F32), jnp.zeros((tq, 1), F32),
            jnp.zeros((tq, v_ref.shape[-1]), F32))
    carry = lax.fori_loop(0, i, body, init)
    off = pl.multiple_of(i * tq, tq)
    r = lax.broadcasted_iota(jnp.int32, (tq, tq), 0)
    c = lax.broadcasted_iota(jnp.int32, (tq, tq), 1)
    m, l, acc = step(k_ref[0, pl.ds(off, tq), :], v_ref[0, pl.ds(off, tq), :], carry, c <= r)
    o_ref[0] = (acc / l).astype(o_ref.dtype)


def _mla_attn(q, k, v, heads):
    B, S, _ = q.shape
    tq = _tile(S, MLA_TQ)
    dv = v.shape[-1] // heads
    return pl.pallas_call(
        functools.partial(_mla_attn_kernel, tq=tq),
        out_shape=jax.ShapeDtypeStruct((B, S, heads * dv), BF16),
        grid=(B, heads, S // tq),
        in_specs=[pl.BlockSpec((1, tq, 2 * LANE), lambda b, h, i: (b, i, h)),
                  pl.BlockSpec((1, S, 2 * LANE), lambda b, h, i: (b, 0, h)),
                  pl.BlockSpec((1, S, dv), lambda b, h, i: (b, 0, h))],
        out_specs=pl.BlockSpec((1, tq, dv), lambda b, h, i: (b, i, h)),
        compiler_params=_cparams(("arbitrary", "arbitrary", "arbitrary")),
        name="mla_attn",
    )(q, k, v)


def _swa_kernel(sink_ref, q_ref, kp_ref, kc_ref, vp_ref, vc_ref, bias_ref, o_ref, *, groups, per):
    hd = SWA_HEAD_DIM
    gw = per * hd
    kband = jnp.concatenate([kp_ref[0], kc_ref[0]], axis=0)
    vband = jnp.concatenate([vp_ref[0], vc_ref[0]], axis=0)
    lane = lax.broadcasted_iota(jnp.int32, (BLOCK, gw), 1)
    sels = [jnp.logical_and(lane >= hh * hd, lane < (hh + 1) * hd) for hh in range(per)]
    rowh = lax.broadcasted_iota(jnp.int32, (per * BLOCK, 1), 0) // BLOCK
    dn = (((1,), (1,)), ((), ()))
    for g in range(groups):
        kg = kband[:, g * hd:(g + 1) * hd]
        vg = vband[:, g * hd:(g + 1) * hd]
        krep = jnp.concatenate([kg] * per, axis=1)
        vrep = jnp.concatenate([vg] * per, axis=1)
        qg = q_ref[0, :, g * gw:(g + 1) * gw]
        qs = jnp.concatenate([jnp.where(sels[hh], qg, jnp.zeros_like(qg)) for hh in range(per)], axis=0)
        s = lax.dot_general(qs, krep, dn, preferred_element_type=F32)
        s = s + bias_ref[0, g * per:(g + 1) * per].reshape(per * BLOCK, 2 * BLOCK)
        sink = jnp.full((per * BLOCK, 1), sink_ref[g * per], F32)
        for hh in range(1, per):
            sink = jnp.where(rowh == hh, sink_ref[g * per + hh], sink)
        m = jnp.maximum(jnp.max(s, axis=-1, keepdims=True), sink)
        p = jnp.exp(s - m)
        den = jnp.sum(p, axis=-1, keepdims=True) + jnp.exp(sink - m)
        o = jnp.dot(p.astype(vrep.dtype), vrep, preferred_element_type=F32) * (1.0 / den)
        og = o[0:BLOCK]
        for hh in range(1, per):
            og = jnp.where(sels[hh], o[hh * BLOCK:(hh + 1) * BLOCK], og)
        o_ref[0, :, g * gw:(g + 1) * gw] = og.astype(o_ref.dtype)


def _swa_attn(proj3, lay, sinks, bias_masked):
    B, S, _ = proj3.shape
    nb = S // BLOCK
    groups, per = SWA_KV_HEADS, SWA_Q_HEADS // SWA_KV_HEADS
    qw = SWA_Q_HEADS * SWA_HEAD_DIM
    kw = SWA_KV_HEADS * SWA_HEAD_DIM
    assert lay["q_s"] % qw == 0 and lay["k_s"] % kw == 0 and lay["v_s"] % kw == 0
    qo, ko, vo = lay["q_s"] // qw, lay["k_s"] // kw, lay["v_s"] // kw
    prev = lambda o: (lambda b, i, s: (b, jnp.maximum(i - 1, 0), o))
    cur = lambda o: (lambda b, i, s: (b, i, o))
    grid_spec = pltpu.PrefetchScalarGridSpec(
        num_scalar_prefetch=1,
        grid=(B, nb),
        in_specs=[pl.BlockSpec((1, BLOCK, qw), cur(qo)),
                  pl.BlockSpec((1, BLOCK, kw), prev(ko)),
                  pl.BlockSpec((1, BLOCK, kw), cur(ko)),
                  pl.BlockSpec((1, BLOCK, kw), prev(vo)),
                  pl.BlockSpec((1, BLOCK, kw), cur(vo)),
                  pl.BlockSpec((1,) + bias_masked.shape[1:], lambda b, i, s: (jnp.minimum(i, 1), 0, 0, 0))],
        out_specs=pl.BlockSpec((1, BLOCK, qw), lambda b, i, s: (b, i, 0)),
    )
    return pl.pallas_call(
        functools.partial(_swa_kernel, groups=groups, per=per),
        out_shape=jax.ShapeDtypeStruct((B, S, qw), BF16),
        grid_spec=grid_spec,
        compiler_params=_cparams(("arbitrary", "arbitrary")),
        name="swa_attn",
    )(sinks, proj3, proj3, proj3, proj3, proj3, bias_masked)


def _merge_kernel(om_ref, os_ref, g0_ref, g1_ref, bg_ref, wm_ref, ws_ref, o_ref):
    ym = jnp.dot(om_ref[...], wm_ref[...], preferred_element_type=F32)
    ys = jnp.dot(os_ref[...], ws_ref[...], preferred_element_type=F32)
    g0 = jax.nn.sigmoid(g0_ref[...].astype(F32) + bg_ref[0:1, :])
    g1 = jax.nn.sigmoid(g1_ref[...].astype(F32) + bg_ref[1:2, :])
    o_ref[...] = (g0 * ym + g1 * ys).astype(o_ref.dtype)


def _merge(om, osw, proj, lay, bg, wm, ws):
    T = om.shape[0]
    D = wm.shape[1]
    tm = _tile(T, MERGE_TM)
    assert lay["g0"] % D == 0 and lay["g1"] % D == 0
    row = lambda w: pl.BlockSpec((tm, w), lambda i: (i, 0))
    full = lambda a: pl.BlockSpec(a.shape, lambda i: (0,) * a.ndim)
    return pl.pallas_call(
        _merge_kernel,
        out_shape=jax.ShapeDtypeStruct((T, D), BF16),
        grid=(T // tm,),
        in_specs=[row(om.shape[1]), row(osw.shape[1]),
                  pl.BlockSpec((tm, D), lambda i, o=lay["g0"] // D: (i, o)),
                  pl.BlockSpec((tm, D), lambda i, o=lay["g1"] // D: (i, o)),
                  full(bg), full(wm), full(ws)],
        out_specs=row(D),
        compiler_params=_cparams(("arbitrary",)),
        name="merge",
    )(om, osw, proj, proj, bg, wm, ws)


def _layer_norm(z, g, b):
    mu = jnp.mean(z, axis=-1, keepdims=True)
    zc = z - mu
    var = jnp.mean(zc * zc, axis=-1, keepdims=True)
    return zc * lax.rsqrt(var + LN_EPS) * g + b


def _pack_pairs(y):
    n = y.shape[1] // 2
    lo = pltpu.bitcast(y[:, :n].astype(BF16).astype(F32), jnp.uint32)
    hi = pltpu.bitcast(y[:, n:].astype(BF16).astype(F32), jnp.uint32)
    return (lo >> 16) | (hi & jnp.uint32(0xFFFF0000))


def _unpack_pairs(w):
    lo = pltpu.bitcast(w << 16, F32)
    hi = pltpu.bitcast(w & jnp.uint32(0xFFFF0000), F32)
    return lo, hi


def _out_ln_kernel(mg_ref, x_ref, wo_ref, g_ref, b_ref, wrt_ref, h_ref, hb_ref, hp_ref, st_ref):
    mix = jnp.dot(mg_ref[...], wo_ref[...], preferred_element_type=F32)
    h = _layer_norm(ALPHA * x_ref[...] + mix, g_ref[...], b_ref[...])
    h_ref[...] = h
    hb = h.astype(BF16)
    hb_ref[...] = hb
    hp_ref[...] = _pack_pairs(h)
    logits_t = lax.dot_general(wrt_ref[...], hb, (((1,), (1,)), ((), ())), preferred_element_type=F32)
    st_ref[...] = jax.nn.sigmoid(logits_t)


def _out_ln(merged, x2, wo, g, b, wrt):
    T, D = x2.shape
    E = wrt.shape[0]
    tm = _tile(T, OUT_TM)
    row = lambda w: pl.BlockSpec((tm, w), lambda i: (i, 0))
    full = lambda a: pl.BlockSpec(a.shape, lambda i: (0,) * a.ndim)
    return pl.pallas_call(
        _out_ln_kernel,
        out_shape=(jax.ShapeDtypeStruct((T, D), F32), jax.ShapeDtypeStruct((T, D), BF16),
                   jax.ShapeDtypeStruct((T, D // 2), jnp.uint32), jax.ShapeDtypeStruct((E, T), F32)),
        grid=(T // tm,),
        in_specs=[row(D), row(D), full(wo), full(g), full(b), full(wrt)],
        out_specs=(row(D), row(D), row(D // 2), pl.BlockSpec((E, tm), lambda i: (0, i))),
        compiler_params=_cparams(("arbitrary",)),
        name="out_ln",
    )(merged, x2, wo, g, b, wrt)


def _router_kernel(s_ref, b_ref, u_ref, te_ref, w_ref, rk_ref, cnt_ref, carry_ref, *, per):
    G = N_GROUPS
    tm = s_ref.shape[1]
    ninf = -jnp.inf

    @pl.when(pl.program_id(0) == 0)
    def _():
        carry_ref[...] = jnp.zeros_like(carry_ref)

    giota = lax.broadcasted_iota(jnp.int32, (G, tm), 0)
    s = [s_ref[j * G:(j + 1) * G, :] for j in range(per)]
    sel = [s[j] + b_ref[j * G:(j + 1) * G, :] for j in range(per)]
    eidx = [giota * per + j for j in range(per)]

    m1 = functools.reduce(jnp.maximum, sel)
    j1 = functools.reduce(jnp.minimum, [jnp.where(sel[j] == m1, j, per) for j in range(per)])
    m2 = functools.reduce(jnp.maximum, [jnp.where(j1 == j, ninf, sel[j]) for j in range(per)])
    grp = m1 + m2
    gmask = jnp.zeros((G, tm), jnp.bool_)
    for _ in range(TOPK_GROUPS):
        mx = jnp.max(grp, axis=0, keepdims=True)
        gi = jnp.min(jnp.where(grp == mx, giota, G), axis=0, keepdims=True)
        chosen = giota == gi
        gmask = jnp.logical_or(gmask, chosen)
        grp = jnp.where(chosen, ninf, grp)

    cur = [jnp.where(gmask, sel[j], ninf) for j in range(per)]
    mem = [jnp.zeros((G, tm), F32) for _ in range(per)]
    e_rows, w_rows = [], []
    for _ in range(TOP_K):
        mx = jnp.max(functools.reduce(jnp.maximum, cur), axis=0, keepdims=True)
        cand = functools.reduce(jnp.minimum, [jnp.where(cur[j] == mx, eidx[j], N_EXPERTS) for j in range(per)])
        emin = jnp.min(cand, axis=0, keepdims=True)
        wk = jnp.zeros((1, tm), F32)
        for j in range(per):
            hit = eidx[j] == emin
            wk = wk + jnp.sum(jnp.where(hit, s[j], 0.0), axis=0, keepdims=True)
            mem[j] = jnp.where(hit, 1.0, mem[j])
            cur[j] = jnp.where(hit, ninf, cur[j])
        e_rows.append(emin)
        w_rows.append(wk)
    wsum = functools.reduce(lambda a, b: a + b, w_rows)

    memall = jnp.concatenate(mem, axis=0)
    pre = jnp.dot(memall.astype(BF16), u_ref[...], preferred_element_type=F32)
    rank = pre + carry_ref[:, 0:1]
    tot = carry_ref[...] + jnp.sum(memall, axis=1, keepdims=True)
    carry_ref[...] = tot
    cnt_ref[...] = tot

    kiota = lax.broadcasted_iota(jnp.int32, (TOP_K, tm), 0)
    te = jnp.zeros((TOP_K, tm), jnp.int32)
    wt = jnp.zeros((TOP_K, tm), F32)
    rk = jnp.zeros((TOP_K, tm), F32)
    for k in range(TOP_K):
        rk_k = jnp.zeros((1, tm), F32)
        for j in range(per):
            rk_k = rk_k + jnp.sum(jnp.where(eidx[j] == e_rows[k], rank[j * G:(j + 1) * G, :], 0.0),
                                  axis=0, keepdims=True)
        te = jnp.where(kiota == k, e_rows[k], te)
        wt = jnp.where(kiota == k, w_rows[k] / wsum * ROUTED_SCALE, wt)
        rk = jnp.where(kiota == k, rk_k, rk)
    te_ref[...] = te
    w_ref[...] = wt
    rk_ref[...] = rk.astype(jnp.int32)


def _router(scores_t, bias_col, utri):
    E, T = scores_t.shape
    tm = utri.shape[0]
    per = E // N_GROUPS
    tk = lambda: pl.BlockSpec((TOP_K, tm), lambda i: (0, i))
    return pl.pallas_call(
        functools.partial(_router_kernel, per=per),
        out_shape=(jax.ShapeDtypeStruct((TOP_K, T), jnp.int32), jax.ShapeDtypeStruct((TOP_K, T), F32),
                   jax.ShapeDtypeStruct((TOP_K, T), jnp.int32), jax.ShapeDtypeStruct((E, LANE), F32)),
        grid=(T // tm,),
        in_specs=[pl.BlockSpec((E, tm), lambda i: (0, i)),
                  pl.BlockSpec((E, 1), lambda i: (0, 0)),
                  pl.BlockSpec((tm, tm), lambda i: (0, 0))],
        out_specs=(tk(), tk(), tk(), pl.BlockSpec((E, LANE), lambda i: (0, 0))),
        scratch_shapes=[pltpu.VMEM((E, LANE), F32)],
        compiler_params=_cparams(("arbitrary",)),
        name="router",
    )(scores_t, bias_col, utri)


def _dispatch_kernel(dest_ref, h_ref, xs_ref, sem):
    tm = h_ref.shape[0]

    def body(r, c):
        for k in range(TOP_K):
            pltpu.make_async_copy(h_ref.at[pl.ds(r, 1)], xs_ref.at[pl.ds(dest_ref[k, r], 1)], sem).start()
        return c

    lax.fori_loop(0, tm, body, 0)
    for k in range(TOP_K):
        pltpu.make_async_copy(h_ref, xs_ref.at[pl.ds(0, tm)], sem).wait()


def _dispatch(dest_t, hp):
    T, W = hp.shape
    tm = _tile(T, DISPATCH_TM)
    return pl.pallas_call(
        _dispatch_kernel,
        out_shape=jax.ShapeDtypeStruct((T * TOP_K, W), hp.dtype),
        grid=(T // tm,),
        in_specs=[pl.BlockSpec((TOP_K, tm), lambda i: (0, i), memory_space=pltpu.SMEM),
                  pl.BlockSpec((tm, W), lambda i: (i, 0))],
        out_specs=pl.BlockSpec(memory_space=pl.ANY),
        scratch_shapes=[pltpu.SemaphoreType.DMA(())],
        compiler_params=_cparams(("arbitrary",)),
        name="dispatch",
    )(dest_t, hp)


def _moe_kernel(ib_ref, ie_ref, lo_ref, hi_ref, n_ref, x_ref, wgu_ref, wd_ref, y_ref, acc_ref, *, ff):
    i = pl.program_id(0)
    n = n_ref[0]
    last_i = ib_ref.shape[0] - 1
    tm = x_ref.shape[0]
    half = x_ref.shape[1]

    @pl.when(i < n)
    def _():
        b = ib_ref[i]
        first = jnp.logical_or(i == 0, ib_ref[jnp.maximum(i - 1, 0)] != b)
        last = jnp.logical_or(i == n - 1, ib_ref[jnp.minimum(i + 1, last_i)] != b)
        xlo, xhi = _unpack_pairs(x_ref[...])
        h = (jnp.dot(xlo.astype(BF16), wgu_ref[0, :half, :], preferred_element_type=F32)
             + jnp.dot(xhi.astype(BF16), wgu_ref[0, half:, :], preferred_element_type=F32))
        a = (jax.nn.silu(h[:, :ff]) * h[:, ff:]).astype(BF16)
        y = jnp.dot(a, wd_ref[0], preferred_element_type=F32)
        rows = lax.broadcasted_iota(jnp.int32, (tm, 1), 0)
        keep = jnp.logical_and(rows >= lo_ref[i], rows < hi_ref[i])
        y = jnp.where(keep, y, 0.0)

        @pl.when(jnp.logical_and(first, last))
        def _():
            y_ref[...] = _pack_pairs(y)

        @pl.when(jnp.logical_and(first, jnp.logical_not(last)))
        def _():
            acc_ref[...] = y

        @pl.when(jnp.logical_not(first))
        def _():
            acc_ref[...] += y

        @pl.when(jnp.logical_and(last, jnp.logical_not(first)))
        def _():
            y_ref[...] = _pack_pairs(acc_ref[...])


def _moe_experts(ib, ie, lo, hi, n_items, xs, wgu, wd, tm):
    N, W = xs.shape
    ff = wd.shape[1]
    D = wd.shape[2]
    grid_spec = pltpu.PrefetchScalarGridSpec(
        num_scalar_prefetch=5,
        grid=(ib.shape[0],),
        in_specs=[pl.BlockSpec((tm, W), lambda i, ib, ie, lo, hi, n: (ib[i], 0)),
                  pl.BlockSpec((1, D, 2 * ff), lambda i, ib, ie, lo, hi, n: (ie[i], 0, 0)),
                  pl.BlockSpec((1, ff, D), lambda i, ib, ie, lo, hi, n: (ie[i], 0, 0))],
        out_specs=pl.BlockSpec((tm, W), lambda i, ib, ie, lo, hi, n: (ib[i], 0)),
        scratch_shapes=[pltpu.VMEM((tm, D), F32)],
    )
    return pl.pallas_call(
        functools.partial(_moe_kernel, ff=ff),
        out_shape=jax.ShapeDtypeStruct((N, W), jnp.uint32),
        grid_spec=grid_spec,
        compiler_params=_cparams(("arbitrary",)),
        name="moe_experts",
    )(ib, ie, lo, hi, n_items, xs, wgu, wd)


def _combine_kernel(dest_ref, dnext_ref, ys_ref, w_ref, h_ref, hb_ref, wsg_ref, wsd_ref, g_ref, b_ref, o_ref,
                    ybuf, routed, sems, *, ff):
    i = pl.program_id(0)
    last = pl.num_programs(0) - 1
    tm = h_ref.shape[0]
    half = ybuf.shape[3]
    sub = 8
    slot = lax.rem(i, 2)
    nslot = 1 - slot

    def issue(idx_ref, r, to_slot):
        for k in range(TOP_K):
            pltpu.make_async_copy(ys_ref.at[pl.ds(idx_ref[k, r], 1)], ybuf.at[to_slot, k, pl.ds(r, 1)],
                                  sems.at[to_slot]).start()

    @pl.when(i == 0)
    def _():
        def first(r, c):
            issue(dest_ref, r, slot)
            return c
        lax.fori_loop(0, tm, first, 0)

    for k in range(TOP_K):
        pltpu.make_async_copy(ys_ref.at[pl.ds(0, tm)], ybuf.at[slot, k], sems.at[slot]).wait()

    def sum_rows(r0):
        w = w_ref[pl.ds(r0, sub), :]
        rlo = jnp.zeros((sub, half), F32)
        rhi = jnp.zeros((sub, half), F32)
        for k in range(TOP_K):
            lo, hi = _unpack_pairs(ybuf[slot, k, pl.ds(r0, sub), :])
            rlo = rlo + lo * w[:, k:k + 1]
            rhi = rhi + hi * w[:, k:k + 1]
        routed[pl.ds(r0, sub), :half] = rlo
        routed[pl.ds(r0, sub), half:] = rhi

    @pl.when(i < last)
    def _():
        def fused(t, c):
            r0 = pl.multiple_of(t * sub, sub)
            sum_rows(r0)
            for j in range(sub):
                issue(dnext_ref, r0 + j, nslot)
            return c
        lax.fori_loop(0, tm // sub, fused, 0)

    @pl.when(i == last)
    def _():
        def tail(t, c):
            sum_rows(pl.multiple_of(t * sub, sub))
            return c
        lax.fori_loop(0, tm // sub, tail, 0)

    s = jnp.dot(hb_ref[...], wsg_ref[...], preferred_element_type=F32)
    a = (jax.nn.silu(s[:, :ff]) * s[:, ff:]).astype(BF16)
    shared = jnp.dot(a, wsd_ref[...], preferred_element_type=F32)
    o_ref[...] = _layer_norm(ALPHA * h_ref[...] + routed[...] + shared, g_ref[...], b_ref[...])


def _moe_combine(dest_t, ys, w, h, hb, wsg, wsd, g, b):
    T, D = h.shape
    ff = wsd.shape[0]
    tm = _tile(T, COMBINE_TM)
    nt = T // tm
    row = lambda wd_: pl.BlockSpec((tm, wd_), lambda i: (i, 0))
    full = lambda a: pl.BlockSpec(a.shape, lambda i: (0,) * a.ndim)
    return pl.pallas_call(
        functools.partial(_combine_kernel, ff=ff),
        out_shape=jax.ShapeDtypeStruct((T, D), F32),
        grid=(T // tm,),
        in_specs=[pl.BlockSpec((TOP_K, tm), lambda i: (0, i), memory_space=pltpu.SMEM),
                  pl.BlockSpec((TOP_K, tm), lambda i: (0, jnp.minimum(i + 1, nt - 1)), memory_space=pltpu.SMEM),
                  pl.BlockSpec(memory_space=pl.ANY), row(TOP_K), row(D), row(D),
                  full(wsg), full(wsd), full(g), full(b)],
        out_specs=row(D),
        scratch_shapes=[pltpu.VMEM((2, TOP_K, tm, ys.shape[1]), ys.dtype), pltpu.VMEM((tm, D), F32),
                        pltpu.SemaphoreType.DMA((2,))],
        compiler_params=_cparams(("arbitrary",)),
        name="moe_combine",
    )(dest_t, dest_t, ys, w, h, hb, wsg, wsd, g, b)


def _rot_cols(w):
    half = w.shape[-1] // 2
    return jnp.concatenate([-w[..., half:], w[..., :half]], axis=-1)


def _t5_bucket(dist):
    n = jnp.maximum(dist, 0)
    max_exact = REL_BUCKETS // 2
    large = max_exact + (jnp.log(jnp.maximum(n, 1).astype(F32) / max_exact)
                         / math.log(REL_MAX_DIST / max_exact) * (REL_BUCKETS - max_exact)).astype(jnp.int32)
    large = jnp.minimum(large, REL_BUCKETS - 1)
    return jnp.where(n < max_exact, n, large)


def _layout(widths):
    order = sorted(range(len(widths)), key=lambda j: -widths[j][1])
    lay, off = {}, 0
    for j in order:
        name, w = widths[j]
        assert off % w == 0, (name, off, w)
        lay[name] = off
        off += w
    return lay, off, [widths[j][0] for j in order]


def _work_items(gs, tm, n_blocks):
    E = gs.shape[0]
    end = jnp.cumsum(gs)
    start = end - gs
    b0 = start // tm
    n_e = jnp.where(gs > 0, (end + tm - 1) // tm - b0, 0)
    icum = jnp.cumsum(n_e)
    n_items = icum[-1]
    cap = n_blocks + E
    idx = jnp.arange(cap, dtype=jnp.int32)
    ie = jnp.minimum(jnp.sum(idx[:, None] >= icum[None, :], axis=1), E - 1).astype(jnp.int32)
    ib = (b0[ie] + idx - (icum - n_e)[ie]).astype(jnp.int32)
    lo = jnp.maximum(start[ie] - ib * tm, 0)
    hi = jnp.minimum(end[ie] - ib * tm, tm)
    ok = idx < n_items
    ib = jnp.where(ok, ib, n_blocks - 1).astype(jnp.int32)
    lo = jnp.where(ok, lo, 0).astype(jnp.int32)
    hi = jnp.where(ok, hi, 0).astype(jnp.int32)
    return start, ib, ie, lo, hi, n_items.astype(jnp.int32).reshape(1)


def kernel(x, positions, w_in, b_gate, q_norm_g, kv_norm_g, w_uq, w_uk, w_uv, swa_sinks, rel_table,
           w_br_mla, w_br_swa, w_out, ln1_g, ln1_b, w_router, router_bias, w_gate_up, w_down,
           w_shared_gate_up, w_shared_down, ln2_g, ln2_b):
    B, S, D = x.shape
    T = B * S
    H = MLA_HEADS
    E = N_EXPERTS
    assert w_in.shape[0] == DEPTH == 1
    qr, kvr = w_uq.shape[1], w_uk.shape[1]
    rd = MLA_ROPE_DIM
    qw = SWA_Q_HEADS * SWA_HEAD_DIM
    kw = SWA_KV_HEADS * SWA_HEAD_DIM

    sizes = (qr, kvr, rd, qw, kw, kw, N_BRANCHES * D)
    bounds = np.cumsum(sizes)[:-1].tolist()
    w_cq, w_ckv, w_kr, w_qs, w_ks, w_vs, w_gate = jnp.split(w_in[0], bounds, axis=-1)
    zpad = jnp.zeros((D, LANE - rd), F32)
    segs = {"g0": w_gate[:, :D], "g1": w_gate[:, D:], "q_s": w_qs * (SWA_HEAD_DIM ** -0.5), "c_q": w_cq, "c_kv": w_ckv,
            "k_s": w_ks, "v_s": w_vs,
            "kra": jnp.concatenate([w_kr, zpad], axis=1),
            "krb": jnp.concatenate([_rot_cols(w_kr), zpad], axis=1)}
    lay, total, order = _layout([(n, int(a.shape[1])) for n, a in segs.items()])
    w_p = jnp.concatenate([segs[n] for n in order], axis=1).astype(BF16)

    wq = w_uq[0].reshape(qr, H, MLA_NOPE_DIM + rd)
    wq_nope, wq_rope = wq[..., :MLA_NOPE_DIM], wq[..., MLA_NOPE_DIM:]
    z64 = jnp.zeros((qr, H, LANE - rd), F32)
    wqa = jnp.concatenate([wq_nope, wq_rope, z64], axis=-1).reshape(qr, H * 2 * LANE).astype(BF16)
    wqb = jnp.concatenate([_rot_cols(wq_rope), z64], axis=-1).reshape(qr, H * LANE).astype(BF16)
    wuk = w_uk[0].astype(BF16)
    wuv = w_uv[0].astype(BF16)

    inv = ROPE_THETA ** (-jnp.arange(0, rd, 2, dtype=F32) / rd)
    ang = positions.astype(F32).reshape(T, 1) * inv
    zc = jnp.zeros((T, LANE - rd), F32)
    cosp = jnp.concatenate([jnp.cos(ang), jnp.cos(ang), zc], axis=1)
    sinp = jnp.concatenate([jnp.sin(ang), jnp.sin(ang), zc], axis=1)

    qi = jnp.arange(BLOCK)[:, None]
    kj = jnp.arange(2 * BLOCK)[None, :]
    dist = qi + BLOCK - kj
    bias = rel_table[_t5_bucket(dist)].astype(F32).transpose(2, 0, 1)
    in_window = (dist >= 0) & (dist < WINDOW)
    bias_masked = jnp.stack([jnp.where((in_window & (kj >= BLOCK))[None], bias, NEG_BIG),
                             jnp.where(in_window[None], bias, NEG_BIG)])

    per = E // N_GROUPS
    perm = np.arange(E).reshape(N_GROUPS, per).T.reshape(-1)
    wrt = w_router[0].T[perm].astype(BF16)
    bias_col = router_bias[0].astype(F32)[perm].reshape(E, 1)
    rt = _tile(T, ROUTER_TM)
    utri = jnp.asarray(np.triu(np.ones((rt, rt), np.float32), k=1), dtype=BF16)

    x2 = x.reshape(T, D)

    proj = _in_proj(x2, w_p)
    q, k, v = _mla_proj(proj, lay, cosp, sinp, q_norm_g[0].reshape(1, qr), kv_norm_g[0].reshape(1, kvr),
                        wqa, wqb, wuk, wuv, H)
    o_m = _mla_attn(q.reshape(B, S, -1), k.reshape(B, S, -1), v.reshape(B, S, -1), H).reshape(T, -1)
    o_s = _swa_attn(proj.reshape(B, S, total), lay, swa_sinks[0].astype(F32), bias_masked).reshape(T, qw)
    merged = _merge(o_m, o_s, proj, lay, b_gate[0], w_br_mla[0].astype(BF16), w_br_swa[0].astype(BF16))
    h, hb, hp, scores_t = _out_ln(merged, x2, w_out[0].astype(BF16), ln1_g[0].reshape(1, D),
                                  ln1_b[0].reshape(1, D), wrt)

    te_t, w_t, rk_t, cnt = _router(scores_t, bias_col, utri)
    inv_perm = np.argsort(perm)
    gs = cnt[:, 0].astype(jnp.int32)[inv_perm]
    tm = MOE_TM
    N = T * TOP_K
    assert N % tm == 0
    start, ib, ie, lo, hi, n_items = _work_items(gs, tm, N // tm)
    dest_t = rk_t + jnp.sum(jnp.where(te_t[:, :, None] == jnp.arange(E, dtype=jnp.int32),
                                      start.astype(jnp.int32), 0), axis=-1)

    xs = _dispatch(dest_t, hp)
    ys = _moe_experts(ib, ie, lo, hi, n_items, xs, w_gate_up[0].astype(BF16), w_down[0].astype(BF16), tm)
    out = _moe_combine(dest_t, ys, w_t.T, h, hb, w_shared_gate_up[0].astype(BF16),
                       w_shared_down[0].astype(BF16), ln2_g[0].reshape(1, D), ln2_b[0].reshape(1, D))
    return out.reshape(B, S, D)
```

```python
import functools
import math

import jax
import jax.numpy as jnp
import numpy as np
from jax import lax
from jax.experimental import pallas as pl
from jax.experimental.pallas import tpu as pltpu

F32 = jnp.float32
BF16 = jnp.bfloat16

MLA_HEADS = 8
MLA_NOPE_DIM = 128
MLA_ROPE_DIM = 64
MLA_V_DIM = 128
ROPE_THETA = 10000.0
SWA_Q_HEADS = 16
SWA_KV_HEADS = 4
SWA_HEAD_DIM = 64
WINDOW = 128
BLOCK = 128
REL_BUCKETS = 32
REL_MAX_DIST = 128
N_BRANCHES = 2
N_EXPERTS = 64
TOP_K = 8
N_GROUPS = 8
TOPK_GROUPS = 4
ROUTED_SCALE = 2.5
DEPTH = 1
ALPHA = (2 * DEPTH) ** 0.25
LN_EPS = 1e-5
RMS_EPS = 1e-6

LANE = 128
NEG_BIG = -1e30

PROJ_TM = 512
MLA_PROJ_TM = 512
MLA_TQ = 512
MERGE_TM = 512
OUT_TM = 512
MOE_TM = 512
COMBINE_TM = 256
ROUTER_TM = 512
DISPATCH_TM = 512
VMEM_LIMIT = 56 * 1024 * 1024


def _cparams(sem):
    return pltpu.CompilerParams(dimension_semantics=sem, vmem_limit_bytes=VMEM_LIMIT)


def _tile(n, t):
    t = min(n, t)
    assert n % t == 0, (n, t)
    return t


def _in_proj_kernel(x_ref, w_ref, o_ref):
    x = x_ref[...].astype(BF16)
    o_ref[...] = jnp.dot(x, w_ref[...], preferred_element_type=F32).astype(o_ref.dtype)


def _in_proj(x2, w_p):
    T, D = x2.shape
    N = w_p.shape[1]
    tn = N // 2 if (N // 2) % LANE == 0 and N % 2 == 0 else N
    tm = _tile(T, PROJ_TM)
    return pl.pallas_call(
        _in_proj_kernel,
        out_shape=jax.ShapeDtypeStruct((T, N), BF16),
        grid=(N // tn, T // tm),
        in_specs=[pl.BlockSpec((tm, D), lambda j, i: (i, 0)),
                  pl.BlockSpec((D, tn), lambda j, i: (0, j))],
        out_specs=pl.BlockSpec((tm, tn), lambda j, i: (i, j)),
        compiler_params=_cparams(("arbitrary", "arbitrary")),
        name="in_proj",
    )(x2, w_p)


def _mla_proj_kernel(cq_ref, ckv_ref, kra_ref, krb_ref, cos_ref, sin_ref, qg_ref, kvg_ref,
                     wqa_ref, wqb_ref, wuk_ref, wuv_ref, q_ref, k_ref, v_ref, *, heads, scale):
    cos = cos_ref[...]
    sin = sin_ref[...]
    cq = cq_ref[...].astype(F32)
    qn = cq * lax.rsqrt(jnp.mean(cq * cq, axis=-1, keepdims=True) + RMS_EPS) * qg_ref[...]
    qn = qn.astype(BF16)
    qa = jnp.dot(qn, wqa_ref[...], preferred_element_type=F32)
    qb = jnp.dot(qn, wqb_ref[...], preferred_element_type=F32)
    ckv = ckv_ref[...].astype(F32)
    cn = ckv * lax.rsqrt(jnp.mean(ckv * ckv, axis=-1, keepdims=True) + RMS_EPS) * kvg_ref[...]
    cn = cn.astype(BF16)
    kn = jnp.dot(cn, wuk_ref[...], preferred_element_type=F32)
    v_ref[...] = jnp.dot(cn, wuv_ref[...], preferred_element_type=F32).astype(v_ref.dtype)
    krope = (kra_ref[...].astype(F32) * cos + krb_ref[...].astype(F32) * sin).astype(k_ref.dtype)
    for h in range(heads):
        lo = h * 2 * LANE
        q_ref[:, lo:lo + LANE] = (qa[:, lo:lo + LANE] * scale).astype(q_ref.dtype)
        q_ref[:, lo + LANE:lo + 2 * LANE] = (
            (qa[:, lo + LANE:lo + 2 * LANE] * cos + qb[:, h * LANE:(h + 1) * LANE] * sin) * scale
        ).astype(q_ref.dtype)
        k_ref[:, lo:lo + LANE] = kn[:, h * LANE:(h + 1) * LANE].astype(k_ref.dtype)
        k_ref[:, lo + LANE:lo + 2 * LANE] = krope


def _mla_proj(proj, lay, cosp, sinp, qg, kvg, wqa, wqb, wuk, wuv, heads):
    T = proj.shape[0]
    tm = _tile(T, MLA_PROJ_TM)
    qr, kvr = wqa.shape[0], wuk.shape[0]

    def col(name, width):
        off = lay[name]
        assert off % width == 0
        return pl.BlockSpec((tm, width), lambda i, o=off // width: (i, o))

    row = lambda w: pl.BlockSpec((tm, w), lambda i: (i, 0))
    full = lambda a: pl.BlockSpec(a.shape, lambda i: (0,) * a.ndim)
    scale = (MLA_NOPE_DIM + MLA_ROPE_DIM) ** -0.5
    return pl.pallas_call(
        functools.partial(_mla_proj_kernel, heads=heads, scale=scale),
        out_shape=(jax.ShapeDtypeStruct((T, heads * 2 * LANE), BF16),
                   jax.ShapeDtypeStruct((T, heads * 2 * LANE), BF16),
                   jax.ShapeDtypeStruct((T, heads * MLA_V_DIM), BF16)),
        grid=(T // tm,),
        in_specs=[col("c_q", qr), col("c_kv", kvr), col("kra", LANE), col("krb", LANE),
                  row(LANE), row(LANE), full(qg), full(kvg), full(wqa), full(wqb), full(wuk), full(wuv)],
        out_specs=(row(heads * 2 * LANE), row(heads * 2 * LANE), row(heads * MLA_V_DIM)),
        compiler_params=_cparams(("arbitrary",)),
        name="mla_proj",
    )(proj, proj, proj, proj, cosp, sinp, qg, kvg, wqa, wqb, wuk, wuv)


def _mla_attn_kernel(q_ref, k_ref, v_ref, o_ref, *, tq, hp, dv):
    i = pl.program_id(2)
    dk = 2 * LANE
    dn = (((1,), (1,)), ((), ()))
    qs = [q_ref[0, :, hh * dk:(hh + 1) * dk] for hh in range(hp)]

    def step(q, kb, vb, carry, mask):
        m, l, acc = carry
        s = lax.dot_general(q, kb, dn, preferred_element_type=F32)
        if mask is not None:
            s = jnp.where(mask, s, NEG_BIG)
        m_new = jnp.maximum(m, jnp.max(s, axis=-1, keepdims=True))
        p = jnp.exp(s - m_new)
        a = jnp.exp(m - m_new)
        l = a * l + jnp.sum(p, axis=-1, keepdims=True)
        acc = a * acc + jnp.dot(p.astype(vb.dtype), vb, preferred_element_type=F32)
        return m_new, l, acc

    def block(off, carries, mask):
        return tuple(step(qs[hh], k_ref[0, pl.ds(off, tq), hh * dk:(hh + 1) * dk],
                          v_ref[0, pl.ds(off, tq), hh * dv:(hh + 1) * dv], carries[hh], mask)
                     for hh in range(hp))

    def body(j, carries):
        return block(pl.multiple_of(j * tq, tq), carries, None)

    init = tuple((jnp.full((tq, 1), NEG_BIG, F32), jnp.zeros((tq, 1), F32), jnp.zeros((tq, dv), F32))
                 for _ in range(hp))
    carries = lax.fori_loop(0, i, body, init)
    r = lax.broadcasted_iota(jnp.int32, (tq, tq), 0)
    c = lax.broadcasted_iota(jnp.int32, (tq, tq), 1)
    carries = block(pl.multiple_of(i * tq, tq), carries, c <= r)
    for hh in range(hp):
        m, l, acc = carries[hh]
        o_ref[0, :, hh * dv:(hh + 1) * dv] = (acc / l).astype(o_ref.dtype)


def _mla_attn(q, k, v, heads):
    B, S, _ = q.shape
    tq = _tile(S, MLA_TQ)
    dv = v.shape[-1] // heads
    hp = 2 if heads % 2 == 0 else 1
    return pl.pallas_call(
        functools.partial(_mla_attn_kernel, tq=tq, hp=hp, dv=dv),
        out_shape=jax.ShapeDtypeStruct((B, S, heads * dv), BF16),
        grid=(B, heads // hp, S // tq),
        in_specs=[pl.BlockSpec((1, tq, hp * 2 * LANE), lambda b, h, i: (b, i, h)),
                  pl.BlockSpec((1, S, hp * 2 * LANE), lambda b, h, i: (b, 0, h)),
                  pl.BlockSpec((1, S, hp * dv), lambda b, h, i: (b, 0, h))],
        out_specs=pl.BlockSpec((1, tq, hp * dv), lambda b, h, i: (b, i, h)),
        compiler_params=_cparams(("arbitrary", "arbitrary", "arbitrary")),
        name="mla_attn",
    )(q, k, v)


def _swa_kernel(sink_ref, q_ref, kp_ref, kc_ref, vp_ref, vc_ref, bias_ref, o_ref, *, groups, per):
    hd = SWA_HEAD_DIM
    gw = per * hd
    kband = jnp.concatenate([kp_ref[0], kc_ref[0]], axis=0)
    vband = jnp.concatenate([vp_ref[0], vc_ref[0]], axis=0)
    lane = lax.broadcasted_iota(jnp.int32, (BLOCK, gw), 1)
    sels = [jnp.logical_and(lane >= hh * hd, lane < (hh + 1) * hd) for hh in range(per)]
    rowh = lax.broadcasted_iota(jnp.int32, (per * BLOCK, 1), 0) // BLOCK
    dn = (((1,), (1,)), ((), ()))
    for g in range(groups):
        kg = kband[:, g * hd:(g + 1) * hd]
        vg = vband[:, g * hd:(g + 1) * hd]
        krep = jnp.concatenate([kg] * per, axis=1)
        vrep = jnp.concatenate([vg] * per, axis=1)
        qg = q_ref[0, :, g * gw:(g + 1) * gw]
        qs = jnp.concatenate([jnp.where(sels[hh], qg, jnp.zeros_like(qg)) for hh in range(per)], axis=0)
        s = lax.dot_general(qs, krep, dn, preferred_element_type=F32)
        s = s + bias_ref[0, g * per:(g + 1) * per].reshape(per * BLOCK, 2 * BLOCK)
        sink = jnp.full((per * BLOCK, 1), sink_ref[g * per], F32)
        for hh in range(1, per):
            sink = jnp.where(rowh == hh, sink_ref[g * per + hh], sink)
        m = jnp.maximum(jnp.max(s, axis=-1, keepdims=True), sink)
        p = jnp.exp(s - m)
        den = jnp.sum(p, axis=-1, keepdims=True) + jnp.exp(sink - m)
        o = jnp.dot(p.astype(vrep.dtype), vrep, preferred_element_type=F32) * (1.0 / den)
        og = o[0:BLOCK]
        for hh in range(1, per):
            og = jnp.where(sels[hh], o[hh * BLOCK:(hh + 1) * BLOCK], og)
        o_ref[0, :, g * gw:(g + 1) * gw] = og.astype(o_ref.dtype)


def _swa_attn(proj3, lay, sinks, bias_masked):
    B, S, _ = proj3.shape
    nb = S // BLOCK
    groups, per = SWA_KV_HEADS, SWA_Q_HEADS // SWA_KV_HEADS
    qw = SWA_Q_HEADS * SWA_HEAD_DIM
    kw = SWA_KV_HEADS * SWA_HEAD_DIM
    assert lay["q_s"] % qw == 0 and lay["k_s"] % kw == 0 and lay["v_s"] % kw == 0
    qo, ko, vo = lay["q_s"] // qw, lay["k_s"] // kw, lay["v_s"] // kw
    prev = lambda o: (lambda b, i, s: (b, jnp.maximum(i - 1, 0), o))
    cur = lambda o: (lambda b, i, s: (b, i, o))
    grid_spec = pltpu.PrefetchScalarGridSpec(
        num_scalar_prefetch=1,
        grid=(B, nb),
        in_specs=[pl.BlockSpec((1, BLOCK, qw), cur(qo)),
                  pl.BlockSpec((1, BLOCK, kw), prev(ko)),
                  pl.BlockSpec((1, BLOCK, kw), cur(ko)),
                  pl.BlockSpec((1, BLOCK, kw), prev(vo)),
                  pl.BlockSpec((1, BLOCK, kw), cur(vo)),
                  pl.BlockSpec((1,) + bias_masked.shape[1:], lambda b, i, s: (jnp.minimum(i, 1), 0, 0, 0))],
        out_specs=pl.BlockSpec((1, BLOCK, qw), lambda b, i, s: (b, i, 0)),
    )
    return pl.pallas_call(
        functools.partial(_swa_kernel, groups=groups, per=per),
        out_shape=jax.ShapeDtypeStruct((B, S, qw), BF16),
        grid_spec=grid_spec,
        compiler_params=_cparams(("arbitrary", "arbitrary")),
        name="swa_attn",
    )(sinks, proj3, proj3, proj3, proj3, proj3, bias_masked)


def _merge_kernel(om_ref, os_ref, g0_ref, g1_ref, bg_ref, wm_ref, ws_ref, o_ref):
    ym = jnp.dot(om_ref[...], wm_ref[...], preferred_element_type=F32)
    ys = jnp.dot(os_ref[...], ws_ref[...], preferred_element_type=F32)
    g0 = jax.nn.sigmoid(g0_ref[...].astype(F32) + bg_ref[0:1, :])
    g1 = jax.nn.sigmoid(g1_ref[...].astype(F32) + bg_ref[1:2, :])
    o_ref[...] = (g0 * ym + g1 * ys).astype(o_ref.dtype)


def _merge(om, osw, proj, lay, bg, wm, ws):
    T = om.shape[0]
    D = wm.shape[1]
    tm = _tile(T, MERGE_TM)
    assert lay["g0"] % D == 0 and lay["g1"] % D == 0
    row = lambda w: pl.BlockSpec((tm, w), lambda i: (i, 0))
    full = lambda a: pl.BlockSpec(a.shape, lambda i: (0,) * a.ndim)
    return pl.pallas_call(
        _merge_kernel,
        out_shape=jax.ShapeDtypeStruct((T, D), BF16),
        grid=(T // tm,),
        in_specs=[row(om.shape[1]), row(osw.shape[1]),
                  pl.BlockSpec((tm, D), lambda i, o=lay["g0"] // D: (i, o)),
                  pl.BlockSpec((tm, D), lambda i, o=lay["g1"] // D: (i, o)),
                  full(bg), full(wm), full(ws)],
        out_specs=row(D),
        compiler_params=_cparams(("arbitrary",)),
        name="merge",
    )(om, osw, proj, proj, bg, wm, ws)


def _layer_norm(z, g, b):
    mu = jnp.mean(z, axis=-1, keepdims=True)
    zc = z - mu
    var = jnp.mean(zc * zc, axis=-1, keepdims=True)
    return zc * lax.rsqrt(var + LN_EPS) * g + b


def _pack_pairs(y):
    n = y.shape[1] // 2
    lo = pltpu.bitcast(y[:, :n].astype(BF16).astype(F32), jnp.uint32)
    hi = pltpu.bitcast(y[:, n:].astype(BF16).astype(F32), jnp.uint32)
    return (lo >> 16) | (hi & jnp.uint32(0xFFFF0000))


def _unpack_pairs(w):
    lo = pltpu.bitcast(w << 16, F32)
    hi = pltpu.bitcast(w & jnp.uint32(0xFFFF0000), F32)
    return lo, hi


def _out_ln_kernel(mg_ref, x_ref, wo_ref, g_ref, b_ref, wrt_ref, h_ref, hb_ref, hp_ref, st_ref):
    mix = jnp.dot(mg_ref[...], wo_ref[...], preferred_element_type=F32)
    h = _layer_norm(ALPHA * x_ref[...] + mix, g_ref[...], b_ref[...])
    h_ref[...] = h
    hb = h.astype(BF16)
    hb_ref[...] = hb
    hp_ref[...] = _pack_pairs(h)
    logits_t = lax.dot_general(wrt_ref[...], hb, (((1,), (1,)), ((), ())), preferred_element_type=F32)
    st_ref[...] = jax.nn.sigmoid(logits_t)


def _out_ln(merged, x2, wo, g, b, wrt):
    T, D = x2.shape
    E = wrt.shape[0]
    tm = _tile(T, OUT_TM)
    row = lambda w: pl.BlockSpec((tm, w), lambda i: (i, 0))
    full = lambda a: pl.BlockSpec(a.shape, lambda i: (0,) * a.ndim)
    return pl.pallas_call(
        _out_ln_kernel,
        out_shape=(jax.ShapeDtypeStruct((T, D), F32), jax.ShapeDtypeStruct((T, D), BF16),
                   jax.ShapeDtypeStruct((T, D // 2), jnp.uint32), jax.ShapeDtypeStruct((E, T), F32)),
        grid=(T // tm,),
        in_specs=[row(D), row(D), full(wo), full(g), full(b), full(wrt)],
        out_specs=(row(D), row(D), row(D // 2), pl.BlockSpec((E, tm), lambda i: (0, i))),
        compiler_params=_cparams(("arbitrary",)),
        name="out_ln",
    )(merged, x2, wo, g, b, wrt)


def _router_kernel(s_ref, b_ref, u_ref, te_ref, w_ref, rk_ref, cnt_ref, carry_ref, *, per):
    G = N_GROUPS
    tm = s_ref.shape[1]
    ninf = -jnp.inf

    @pl.when(pl.program_id(0) == 0)
    def _():
        carry_ref[...] = jnp.zeros_like(carry_ref)

    giota = lax.broadcasted_iota(jnp.int32, (G, tm), 0)
    s = [s_ref[j * G:(j + 1) * G, :] for j in range(per)]
    sel = [s[j] + b_ref[j * G:(j + 1) * G, :] for j in range(per)]
    eidx = [giota * per + j for j in range(per)]

    m1 = functools.reduce(jnp.maximum, sel)
    j1 = functools.reduce(jnp.minimum, [jnp.where(sel[j] == m1, j, per) for j in range(per)])
    m2 = functools.reduce(jnp.maximum, [jnp.where(j1 == j, ninf, sel[j]) for j in range(per)])
    grp = m1 + m2
    gmask = jnp.zeros((G, tm), jnp.bool_)
    for _ in range(TOPK_GROUPS):
        mx = jnp.max(grp, axis=0, keepdims=True)
        gi = jnp.min(jnp.where(grp == mx, giota, G), axis=0, keepdims=True)
        chosen = giota == gi
        gmask = jnp.logical_or(gmask, chosen)
        grp = jnp.where(chosen, ninf, grp)

    cur = [jnp.where(gmask, sel[j], ninf) for j in range(per)]
    mem = [jnp.zeros((G, tm), F32) for _ in range(per)]
    e_rows, w_rows = [], []
    for _ in range(TOP_K):
        mx = jnp.max(functools.reduce(jnp.maximum, cur), axis=0, keepdims=True)
        cand = functools.reduce(jnp.minimum, [jnp.where(cur[j] == mx, eidx[j], N_EXPERTS) for j in range(per)])
        emin = jnp.min(cand, axis=0, keepdims=True)
        wk = jnp.zeros((1, tm), F32)
        for j in range(per):
            hit = eidx[j] == emin
            wk = wk + jnp.sum(jnp.where(hit, s[j], 0.0), axis=0, keepdims=True)
            mem[j] = jnp.where(hit, 1.0, mem[j])
            cur[j] = jnp.where(hit, ninf, cur[j])
        e_rows.append(emin)
        w_rows.append(wk)
    wsum = functools.reduce(lambda a, b: a + b, w_rows)

    memall = jnp.concatenate(mem, axis=0)
    pre = jnp.dot(memall.astype(BF16), u_ref[...], preferred_element_type=F32)
    rank = pre + carry_ref[:, 0:1]
    tot = carry_ref[...] + jnp.sum(memall, axis=1, keepdims=True)
    carry_ref[...] = tot
    cnt_ref[...] = tot

    kiota = lax.broadcasted_iota(jnp.int32, (TOP_K, tm), 0)
    te = jnp.zeros((TOP_K, tm), jnp.int32)
    wt = jnp.zeros((TOP_K, tm), F32)
    rk = jnp.zeros((TOP_K, tm), F32)
    for k in range(TOP_K):
        rk_k = jnp.zeros((1, tm), F32)
        for j in range(per):
            rk_k = rk_k + jnp.sum(jnp.where(eidx[j] == e_rows[k], rank[j * G:(j + 1) * G, :], 0.0),
                                  axis=0, keepdims=True)
        te = jnp.where(kiota == k, e_rows[k], te)
        wt = jnp.where(kiota == k, w_rows[k] / wsum * ROUTED_SCALE, wt)
        rk = jnp.where(kiota == k, rk_k, rk)
    te_ref[...] = te
    w_ref[...] = wt
    rk_ref[...] = rk.astype(jnp.int32)


def _router(scores_t, bias_col, utri):
    E, T = scores_t.shape
    tm = utri.shape[0]
    per = E // N_GROUPS
    tk = lambda: pl.BlockSpec((TOP_K, tm), lambda i: (0, i))
    return pl.pallas_call(
        functools.partial(_router_kernel, per=per),
        out_shape=(jax.ShapeDtypeStruct((TOP_K, T), jnp.int32), jax.ShapeDtypeStruct((TOP_K, T), F32),
                   jax.ShapeDtypeStruct((TOP_K, T), jnp.int32), jax.ShapeDtypeStruct((E, LANE), F32)),
        grid=(T // tm,),
        in_specs=[pl.BlockSpec((E, tm), lambda i: (0, i)),
                  pl.BlockSpec((E, 1), lambda i: (0, 0)),
                  pl.BlockSpec((tm, tm), lambda i: (0, 0))],
        out_specs=(tk(), tk(), tk(), pl.BlockSpec((E, LANE), lambda i: (0, 0))),
        scratch_shapes=[pltpu.VMEM((E, LANE), F32)],
        compiler_params=_cparams(("arbitrary",)),
        name="router",
    )(scores_t, bias_col, utri)


def _dispatch_kernel(dest_ref, h_ref, xs_ref, sem):
    tm = h_ref.shape[0]

    def body(r, c):
        for k in range(TOP_K):
            pltpu.make_async_copy(h_ref.at[pl.ds(r, 1)], xs_ref.at[pl.ds(dest_ref[k, r], 1)], sem).start()
        return c

    lax.fori_loop(0, tm, body, 0)
    for k in range(TOP_K):
        pltpu.make_async_copy(h_ref, xs_ref.at[pl.ds(0, tm)], sem).wait()


def _dispatch(dest_t, hp):
    T, W = hp.shape
    tm = _tile(T, DISPATCH_TM)
    return pl.pallas_call(
        _dispatch_kernel,
        out_shape=jax.ShapeDtypeStruct((T * TOP_K, W), hp.dtype),
        grid=(T // tm,),
        in_specs=[pl.BlockSpec((TOP_K, tm), lambda i: (0, i), memory_space=pltpu.SMEM),
                  pl.BlockSpec((tm, W), lambda i: (i, 0))],
        out_specs=pl.BlockSpec(memory_space=pl.ANY),
        scratch_shapes=[pltpu.SemaphoreType.DMA(())],
        compiler_params=_cparams(("arbitrary",)),
        name="dispatch",
    )(dest_t, hp)


def _moe_kernel(ib_ref, ie_ref, lo_ref, hi_ref, n_ref, x_ref, wgu_ref, wd_ref, y_ref, acc_ref,
                wgu_b, wd_b, *, ff):
    i = pl.program_id(0)
    n = n_ref[0]
    last_i = ib_ref.shape[0] - 1
    tm = x_ref.shape[0]
    half = x_ref.shape[1]

    @pl.when(i < n)
    def _():
        b = ib_ref[i]
        first = jnp.logical_or(i == 0, ib_ref[jnp.maximum(i - 1, 0)] != b)
        last = jnp.logical_or(i == n - 1, ib_ref[jnp.minimum(i + 1, last_i)] != b)
        xlo, xhi = _unpack_pairs(x_ref[...])
        @pl.when(jnp.logical_or(i == 0, ie_ref[jnp.maximum(i - 1, 0)] != ie_ref[i]))
        def _():
            wgu_b[...] = wgu_ref[0].astype(BF16)
            wd_b[...] = wd_ref[0].astype(BF16)

        h = (jnp.dot(xlo.astype(BF16), wgu_b[:half, :], preferred_element_type=F32)
             + jnp.dot(xhi.astype(BF16), wgu_b[half:, :], preferred_element_type=F32))
        a = (jax.nn.silu(h[:, :ff]) * h[:, ff:]).astype(BF16)
        y = jnp.dot(a, wd_b[...], preferred_element_type=F32)
        rows = lax.broadcasted_iota(jnp.int32, (tm, 1), 0)
        keep = jnp.logical_and(rows >= lo_ref[i], rows < hi_ref[i])
        y = jnp.where(keep, y, 0.0)

        @pl.when(jnp.logical_and(first, last))
        def _():
            y_ref[...] = _pack_pairs(y)

        @pl.when(jnp.logical_and(first, jnp.logical_not(last)))
        def _():
            acc_ref[...] = y

        @pl.when(jnp.logical_not(first))
        def _():
            acc_ref[...] += y

        @pl.when(jnp.logical_and(last, jnp.logical_not(first)))
        def _():
            y_ref[...] = _pack_pairs(acc_ref[...])


def _moe_experts(ib, ie, lo, hi, n_items, xs, wgu, wd, tm):
    N, W = xs.shape
    ff = wd.shape[1]
    D = wd.shape[2]
    grid_spec = pltpu.PrefetchScalarGridSpec(
        num_scalar_prefetch=5,
        grid=(ib.shape[0],),
        in_specs=[pl.BlockSpec((tm, W), lambda i, ib, ie, lo, hi, n: (ib[i], 0)),
                  pl.BlockSpec((1, D, 2 * ff), lambda i, ib, ie, lo, hi, n: (ie[i], 0, 0)),
                  pl.BlockSpec((1, ff, D), lambda i, ib, ie, lo, hi, n: (ie[i], 0, 0))],
        out_specs=pl.BlockSpec((tm, W), lambda i, ib, ie, lo, hi, n: (ib[i], 0)),
        scratch_shapes=[pltpu.VMEM((tm, D), F32), pltpu.VMEM((D, 2 * ff), BF16), pltpu.VMEM((ff, D), BF16)],
    )
    return pl.pallas_call(
        functools.partial(_moe_kernel, ff=ff),
        out_shape=jax.ShapeDtypeStruct((N, W), jnp.uint32),
        grid_spec=grid_spec,
        compiler_params=_cparams(("arbitrary",)),
        name="moe_experts",
    )(ib, ie, lo, hi, n_items, xs, wgu, wd)


def _combine_kernel(dest_ref, dnext_ref, ys_ref, w_ref, h_ref, hb_ref, wsg_ref, wsd_ref, g_ref, b_ref, o_ref,
                    ybuf, routed, sems, *, ff):
    i = pl.program_id(0)
    last = pl.num_programs(0) - 1
    tm = h_ref.shape[0]
    half = ybuf.shape[3]
    sub = 8
    slot = lax.rem(i, 2)
    nslot = 1 - slot

    def issue(idx_ref, r, to_slot):
        for k in range(TOP_K):
            pltpu.make_async_copy(ys_ref.at[pl.ds(idx_ref[k, r], 1)], ybuf.at[to_slot, k, pl.ds(r, 1)],
                                  sems.at[to_slot]).start()

    @pl.when(i == 0)
    def _():
        def first(r, c):
            issue(dest_ref, r, slot)
            return c
        lax.fori_loop(0, tm, first, 0)

    for k in range(TOP_K):
        pltpu.make_async_copy(ys_ref.at[pl.ds(0, tm)], ybuf.at[slot, k], sems.at[slot]).wait()

    def sum_rows(r0):
        w = w_ref[pl.ds(r0, sub), :]
        rlo = jnp.zeros((sub, half), F32)
        rhi = jnp.zeros((sub, half), F32)
        for k in range(TOP_K):
            lo, hi = _unpack_pairs(ybuf[slot, k, pl.ds(r0, sub), :])
            rlo = rlo + lo * w[:, k:k + 1]
            rhi = rhi + hi * w[:, k:k + 1]
        routed[pl.ds(r0, sub), :half] = rlo
        routed[pl.ds(r0, sub), half:] = rhi

    @pl.when(i < last)
    def _():
        def fused(t, c):
            r0 = pl.multiple_of(t * sub, sub)
            sum_rows(r0)
            for j in range(sub):
                issue(dnext_ref, r0 + j, nslot)
            return c
        lax.fori_loop(0, tm // sub, fused, 0)

    @pl.when(i == last)
    def _():
        def tail(t, c):
            sum_rows(pl.multiple_of(t * sub, sub))
            return c
        lax.fori_loop(0, tm // sub, tail, 0)

    s = jnp.dot(hb_ref[...], wsg_ref[...], preferred_element_type=F32)
    a = (jax.nn.silu(s[:, :ff]) * s[:, ff:]).astype(BF16)
    shared = jnp.dot(a, wsd_ref[...], preferred_element_type=F32)
    o_ref[...] = _layer_norm(ALPHA * h_ref[...] + routed[...] + shared, g_ref[...], b_ref[...])


def _moe_combine(dest_t, ys, w, h, hb, wsg, wsd, g, b):
    T, D = h.shape
    ff = wsd.shape[0]
    tm = _tile(T, COMBINE_TM)
    nt = T // tm
    row = lambda wd_: pl.BlockSpec((tm, wd_), lambda i: (i, 0))
    full = lambda a: pl.BlockSpec(a.shape, lambda i: (0,) * a.ndim)
    return pl.pallas_call(
        functools.partial(_combine_kernel, ff=ff),
        out_shape=jax.ShapeDtypeStruct((T, D), F32),
        grid=(T // tm,),
        in_specs=[pl.BlockSpec((TOP_K, tm), lambda i: (0, i), memory_space=pltpu.SMEM),
                  pl.BlockSpec((TOP_K, tm), lambda i: (0, jnp.minimum(i + 1, nt - 1)), memory_space=pltpu.SMEM),
                  pl.BlockSpec(memory_space=pl.ANY), row(TOP_K), row(D), row(D),
                  full(wsg), full(wsd), full(g), full(b)],
        out_specs=row(D),
        scratch_shapes=[pltpu.VMEM((2, TOP_K, tm, ys.shape[1]), ys.dtype), pltpu.VMEM((tm, D), F32),
                        pltpu.SemaphoreType.DMA((2,))],
        compiler_params=_cparams(("arbitrary",)),
        name="moe_combine",
    )(dest_t, dest_t, ys, w, h, hb, wsg, wsd, g, b)


def _rot_cols(w):
    half = w.shape[-1] // 2
    return jnp.concatenate([-w[..., half:], w[..., :half]], axis=-1)


def _t5_bucket(dist):
    n = jnp.maximum(dist, 0)
    max_exact = REL_BUCKETS // 2
    large = max_exact + (jnp.log(jnp.maximum(n, 1).astype(F32) / max_exact)
                         / math.log(REL_MAX_DIST / max_exact) * (REL_BUCKETS - max_exact)).astype(jnp.int32)
    large = jnp.minimum(large, REL_BUCKETS - 1)
    return jnp.where(n < max_exact, n, large)


def _layout(widths):
    order = sorted(range(len(widths)), key=lambda j: -widths[j][1])
    lay, off = {}, 0
    for j in order:
        name, w = widths[j]
        assert off % w == 0, (name, off, w)
        lay[name] = off
        off += w
    return lay, off, [widths[j][0] for j in order]


def _work_items(gs, tm, n_blocks):
    E = gs.shape[0]
    end = jnp.cumsum(gs)
    start = end - gs
    b0 = start // tm
    n_e = jnp.where(gs > 0, (end + tm - 1) // tm - b0, 0)
    icum = jnp.cumsum(n_e)
    n_items = icum[-1]
    cap = n_blocks + E
    idx = jnp.arange(cap, dtype=jnp.int32)
    ie = jnp.minimum(jnp.sum(idx[:, None] >= icum[None, :], axis=1), E - 1).astype(jnp.int32)
    ib = (b0[ie] + idx - (icum - n_e)[ie]).astype(jnp.int32)
    lo = jnp.maximum(start[ie] - ib * tm, 0)
    hi = jnp.minimum(end[ie] - ib * tm, tm)
    ok = idx < n_items
    ib = jnp.where(ok, ib, n_blocks - 1).astype(jnp.int32)
    lo = jnp.where(ok, lo, 0).astype(jnp.int32)
    hi = jnp.where(ok, hi, 0).astype(jnp.int32)
    return start, ib, ie, lo, hi, n_items.astype(jnp.int32).reshape(1)


def kernel(x, positions, w_in, b_gate, q_norm_g, kv_norm_g, w_uq, w_uk, w_uv, swa_sinks, rel_table,
           w_br_mla, w_br_swa, w_out, ln1_g, ln1_b, w_router, router_bias, w_gate_up, w_down,
           w_shared_gate_up, w_shared_down, ln2_g, ln2_b):
    B, S, D = x.shape
    T = B * S
    H = MLA_HEADS
    E = N_EXPERTS
    assert w_in.shape[0] == DEPTH == 1
    qr, kvr = w_uq.shape[1], w_uk.shape[1]
    rd = MLA_ROPE_DIM
    qw = SWA_Q_HEADS * SWA_HEAD_DIM
    kw = SWA_KV_HEADS * SWA_HEAD_DIM

    sizes = (qr, kvr, rd, qw, kw, kw, N_BRANCHES * D)
    bounds = np.cumsum(sizes)[:-1].tolist()
    w_cq, w_ckv, w_kr, w_qs, w_ks, w_vs, w_gate = jnp.split(w_in[0], bounds, axis=-1)
    zpad = jnp.zeros((D, LANE - rd), F32)
    segs = {"g0": w_gate[:, :D], "g1": w_gate[:, D:], "q_s": w_qs * (SWA_HEAD_DIM ** -0.5), "c_q": w_cq, "c_kv": w_ckv,
            "k_s": w_ks, "v_s": w_vs,
            "kra": jnp.concatenate([w_kr, zpad], axis=1),
            "krb": jnp.concatenate([_rot_cols(w_kr), zpad], axis=1)}
    lay, total, order = _layout([(n, int(a.shape[1])) for n, a in segs.items()])
    w_p = jnp.concatenate([segs[n] for n in order], axis=1).astype(BF16)

    wq = w_uq[0].reshape(qr, H, MLA_NOPE_DIM + rd)
    wq_nope, wq_rope = wq[..., :MLA_NOPE_DIM], wq[..., MLA_NOPE_DIM:]
    z64 = jnp.zeros((qr, H, LANE - rd), F32)
    wqa = jnp.concatenate([wq_nope, wq_rope, z64], axis=-1).reshape(qr, H * 2 * LANE).astype(BF16)
    wqb = jnp.concatenate([_rot_cols(wq_rope), z64], axis=-1).reshape(qr, H * LANE).astype(BF16)
    wuk = w_uk[0].astype(BF16)
    wuv = w_uv[0].astype(BF16)

    inv = ROPE_THETA ** (-jnp.arange(0, rd, 2, dtype=F32) / rd)
    ang = positions.astype(F32).reshape(T, 1) * inv
    zc = jnp.zeros((T, LANE - rd), F32)
    cosp = jnp.concatenate([jnp.cos(ang), jnp.cos(ang), zc], axis=1)
    sinp = jnp.concatenate([jnp.sin(ang), jnp.sin(ang), zc], axis=1)

    qi = jnp.arange(BLOCK)[:, None]
    kj = jnp.arange(2 * BLOCK)[None, :]
    dist = qi + BLOCK - kj
    bias = rel_table[_t5_bucket(dist)].astype(F32).transpose(2, 0, 1)
    in_window = (dist >= 0) & (dist < WINDOW)
    bias_masked = jnp.stack([jnp.where((in_window & (kj >= BLOCK))[None], bias, NEG_BIG),
                             jnp.where(in_window[None], bias, NEG_BIG)])

    per = E // N_GROUPS
    perm = np.arange(E).reshape(N_GROUPS, per).T.reshape(-1)
    wrt = w_router[0].T[perm].astype(BF16)
    bias_col = router_bias[0].astype(F32)[perm].reshape(E, 1)
    rt = _tile(T, ROUTER_TM)
    utri = jnp.asarray(np.triu(np.ones((rt, rt), np.float32), k=1), dtype=BF16)

    x2 = x.reshape(T, D)

    proj = _in_proj(x2, w_p)
    q, k, v = _mla_proj(proj, lay, cosp, sinp, q_norm_g[0].reshape(1, qr), kv_norm_g[0].reshape(1, kvr),
                        wqa, wqb, wuk, wuv, H)
    o_m = _mla_attn(q.reshape(B, S, -1), k.reshape(B, S, -1), v.reshape(B, S, -1), H).reshape(T, -1)
    o_s = _swa_attn(proj.reshape(B, S, total), lay, swa_sinks[0].astype(F32), bias_masked).reshape(T, qw)
    merged = _merge(o_m, o_s, proj, lay, b_gate[0], w_br_mla[0].astype(BF16), w_br_swa[0].astype(BF16))
    h, hb, hp, scores_t = _out_ln(merged, x2, w_out[0].astype(BF16), ln1_g[0].reshape(1, D),
                                  ln1_b[0].reshape(1, D), wrt)

    te_t, w_t, rk_t, cnt = _router(scores_t, bias_col, utri)
    inv_perm = np.argsort(perm)
    gs = cnt[:, 0].astype(jnp.int32)[inv_perm]
    tm = MOE_TM
    N = T * TOP_K
    assert N % tm == 0
    start, ib, ie, lo, hi, n_items = _work_items(gs, tm, N // tm)
    dest_t = rk_t + jnp.sum(jnp.where(te_t[:, :, None] == jnp.arange(E, dtype=jnp.int32),
                                      start.astype(jnp.int32), 0), axis=-1)

    xs = _dispatch(dest_t, hp)
    ys = _moe_experts(ib, ie, lo, hi, n_items, xs, w_gate_up[0], w_down[0], tm)
    out = _moe_combine(dest_t, ys, w_t.T, h, hb, w_shared_gate_up[0].astype(BF16),
                       w_shared_down[0].astype(BF16), ln2_g[0].reshape(1, D), ln2_b[0].reshape(1, D))
    return out.reshape(B, S, D)
```

```python
import functools
import math

import jax
import jax.numpy as jnp
import numpy as np
from jax import lax
from jax.experimental import pallas as pl
from jax.experimental.pallas import tpu as pltpu

F32 = jnp.float32
BF16 = jnp.bfloat16

MLA_HEADS = 8
MLA_NOPE_DIM = 128
MLA_ROPE_DIM = 64
MLA_V_DIM = 128
ROPE_THETA = 10000.0
SWA_Q_HEADS = 16
SWA_KV_HEADS = 4
SWA_HEAD_DIM = 64
WINDOW = 128
BLOCK = 128
REL_BUCKETS = 32
REL_MAX_DIST = 128
N_BRANCHES = 2
N_EXPERTS = 64
TOP_K = 8
N_GROUPS = 8
TOPK_GROUPS = 4
ROUTED_SCALE = 2.5
DEPTH = 1
ALPHA = (2 * DEPTH) ** 0.25
LN_EPS = 1e-5
RMS_EPS = 1e-6

LANE = 128
NEG_BIG = -1e30

PROJ_TM = 512
MLA_PROJ_TM = 512
MLA_TQ = 512
MERGE_TM = 512
OUT_TM = 512
MOE_TM = 512
COMBINE_TM = 256
ROUTER_TM = 512
DISPATCH_TM = 512
VMEM_LIMIT = 56 * 1024 * 1024


def _cparams(sem):
    return pltpu.CompilerParams(dimension_semantics=sem, vmem_limit_bytes=VMEM_LIMIT)


def _tile(n, t):
    t = min(n, t)
    assert n % t == 0, (n, t)
    return t


def _in_proj_kernel(x_ref, w_ref, o_ref):
    x = x_ref[...].astype(BF16)
    o_ref[...] = jnp.dot(x, w_ref[...], preferred_element_type=F32).astype(o_ref.dtype)


def _in_proj(x2, w_p):
    T, D = x2.shape
    N = w_p.shape[1]
    tn = N // 2 if (N // 2) % LANE == 0 and N % 2 == 0 else N
    tm = _tile(T, PROJ_TM)
    return pl.pallas_call(
        _in_proj_kernel,
        out_shape=jax.ShapeDtypeStruct((T, N), BF16),
        grid=(N // tn, T // tm),
        in_specs=[pl.BlockSpec((tm, D), lambda j, i: (i, 0)),
                  pl.BlockSpec((D, tn), lambda j, i: (0, j))],
        out_specs=pl.BlockSpec((tm, tn), lambda j, i: (i, j)),
        compiler_params=_cparams(("arbitrary", "arbitrary")),
        name="in_proj",
    )(x2, w_p)


def _mla_proj_kernel(cq_ref, ckv_ref, kra_ref, krb_ref, cos_ref, sin_ref, qg_ref, kvg_ref,
                     wqa_ref, wqb_ref, wuk_ref, wuv_ref, q_ref, k_ref, v_ref, *, heads, scale):
    cos = cos_ref[...]
    sin = sin_ref[...]
    cq = cq_ref[...].astype(F32)
    qn = cq * lax.rsqrt(jnp.mean(cq * cq, axis=-1, keepdims=True) + RMS_EPS) * qg_ref[...]
    qn = qn.astype(BF16)
    qa = jnp.dot(qn, wqa_ref[...], preferred_element_type=F32)
    qb = jnp.dot(qn, wqb_ref[...], preferred_element_type=F32)
    ckv = ckv_ref[...].astype(F32)
    cn = ckv * lax.rsqrt(jnp.mean(ckv * ckv, axis=-1, keepdims=True) + RMS_EPS) * kvg_ref[...]
    cn = cn.astype(BF16)
    kn = jnp.dot(cn, wuk_ref[...], preferred_element_type=F32)
    v_ref[...] = jnp.dot(cn, wuv_ref[...], preferred_element_type=F32).astype(v_ref.dtype)
    krope = (kra_ref[...].astype(F32) * cos + krb_ref[...].astype(F32) * sin).astype(k_ref.dtype)
    for h in range(heads):
        lo = h * 2 * LANE
        q_ref[:, lo:lo + LANE] = (qa[:, lo:lo + LANE] * scale).astype(q_ref.dtype)
        q_ref[:, lo + LANE:lo + 2 * LANE] = (
            (qa[:, lo + LANE:lo + 2 * LANE] * cos + qb[:, h * LANE:(h + 1) * LANE] * sin) * scale
        ).astype(q_ref.dtype)
        k_ref[:, lo:lo + LANE] = kn[:, h * LANE:(h + 1) * LANE].astype(k_ref.dtype)
        k_ref[:, lo + LANE:lo + 2 * LANE] = krope


def _mla_proj(proj, lay, cosp, sinp, qg, kvg, wqa, wqb, wuk, wuv, heads):
    T = proj.shape[0]
    tm = _tile(T, MLA_PROJ_TM)
    qr, kvr = wqa.shape[0], wuk.shape[0]

    def col(name, width):
        off = lay[name]
        assert off % width == 0
        return pl.BlockSpec((tm, width), lambda i, o=off // width: (i, o))

    row = lambda w: pl.BlockSpec((tm, w), lambda i: (i, 0))
    full = lambda a: pl.BlockSpec(a.shape, lambda i: (0,) * a.ndim)
    scale = (MLA_NOPE_DIM + MLA_ROPE_DIM) ** -0.5
    return pl.pallas_call(
        functools.partial(_mla_proj_kernel, heads=heads, scale=scale),
        out_shape=(jax.ShapeDtypeStruct((T, heads * 2 * LANE), BF16),
                   jax.ShapeDtypeStruct((T, heads * 2 * LANE), BF16),
                   jax.ShapeDtypeStruct((T, heads * MLA_V_DIM), BF16)),
        grid=(T // tm,),
        in_specs=[col("c_q", qr), col("c_kv", kvr), col("kra", LANE), col("krb", LANE),
                  row(LANE), row(LANE), full(qg), full(kvg), full(wqa), full(wqb), full(wuk), full(wuv)],
        out_specs=(row(heads * 2 * LANE), row(heads * 2 * LANE), row(heads * MLA_V_DIM)),
        compiler_params=_cparams(("arbitrary",)),
        name="mla_proj",
    )(proj, proj, proj, proj, cosp, sinp, qg, kvg, wqa, wqb, wuk, wuv)


def _mla_attn_kernel(q_ref, k_ref, v_ref, o_ref, *, tq, hp, dv):
    i = pl.program_id(2)
    dk = 2 * LANE
    dn = (((1,), (1,)), ((), ()))
    qs = [q_ref[0, :, hh * dk:(hh + 1) * dk] for hh in range(hp)]

    def step(q, kb, vb, carry, mask):
        m, l, acc = carry
        s = lax.dot_general(q, kb, dn, preferred_element_type=F32)
        if mask is not None:
            s = jnp.where(mask, s, NEG_BIG)
        m_new = jnp.maximum(m, jnp.max(s, axis=-1, keepdims=True))
        p = jnp.exp(s - m_new)
        a = jnp.exp(m - m_new)
        l = a * l + jnp.sum(p, axis=-1, keepdims=True)
        acc = a * acc + jnp.dot(p.astype(vb.dtype), vb, preferred_element_type=F32)
        return m_new, l, acc

    def block(off, carries, mask):
        return tuple(step(qs[hh], k_ref[0, pl.ds(off, tq), hh * dk:(hh + 1) * dk],
                          v_ref[0, pl.ds(off, tq), hh * dv:(hh + 1) * dv], carries[hh], mask)
                     for hh in range(hp))

    def body(j, carries):
        return block(pl.multiple_of(j * tq, tq), carries, None)

    init = tuple((jnp.full((tq, 1), NEG_BIG, F32), jnp.zeros((tq, 1), F32), jnp.zeros((tq, dv), F32))
                 for _ in range(hp))
    carries = lax.fori_loop(0, i, body, init)
    r = lax.broadcasted_iota(jnp.int32, (tq, tq), 0)
    c = lax.broadcasted_iota(jnp.int32, (tq, tq), 1)
    carries = block(pl.multiple_of(i * tq, tq), carries, c <= r)
    for hh in range(hp):
        m, l, acc = carries[hh]
        o_ref[0, :, hh * dv:(hh + 1) * dv] = (acc / l).astype(o_ref.dtype)


def _mla_attn(q, k, v, heads):
    B, S, _ = q.shape
    tq = _tile(S, MLA_TQ)
    dv = v.shape[-1] // heads
    hp = 2 if heads % 2 == 0 else 1
    return pl.pallas_call(
        functools.partial(_mla_attn_kernel, tq=tq, hp=hp, dv=dv),
        out_shape=jax.ShapeDtypeStruct((B, S, heads * dv), BF16),
        grid=(B, heads // hp, S // tq),
        in_specs=[pl.BlockSpec((1, tq, hp * 2 * LANE), lambda b, h, i: (b, i, h)),
                  pl.BlockSpec((1, S, hp * 2 * LANE), lambda b, h, i: (b, 0, h)),
                  pl.BlockSpec((1, S, hp * dv), lambda b, h, i: (b, 0, h))],
        out_specs=pl.BlockSpec((1, tq, hp * dv), lambda b, h, i: (b, i, h)),
        compiler_params=_cparams(("arbitrary", "arbitrary", "arbitrary")),
        name="mla_attn",
    )(q, k, v)


def _swa_kernel(sink_ref, q_ref, kp_ref, kc_ref, vp_ref, vc_ref, bias_ref, o_ref, *, groups, per):
    hd = SWA_HEAD_DIM
    gw = per * hd
    kband = jnp.concatenate([kp_ref[0], kc_ref[0]], axis=0)
    vband = jnp.concatenate([vp_ref[0], vc_ref[0]], axis=0)
    lane = lax.broadcasted_iota(jnp.int32, (BLOCK, gw), 1)
    sels = [jnp.logical_and(lane >= hh * hd, lane < (hh + 1) * hd) for hh in range(per)]
    rowh = lax.broadcasted_iota(jnp.int32, (per * BLOCK, 1), 0) // BLOCK
    dn = (((1,), (1,)), ((), ()))
    for g in range(groups):
        kg = kband[:, g * hd:(g + 1) * hd]
        vg = vband[:, g * hd:(g + 1) * hd]
        krep = jnp.concatenate([kg] * per, axis=1)
        vrep = jnp.concatenate([vg] * per, axis=1)
        qg = q_ref[0, :, g * gw:(g + 1) * gw]
        qs = jnp.concatenate([jnp.where(sels[hh], qg, jnp.zeros_like(qg)) for hh in range(per)], axis=0)
        s = lax.dot_general(qs, krep, dn, preferred_element_type=F32)
        s = s + bias_ref[0, g * per:(g + 1) * per].reshape(per * BLOCK, 2 * BLOCK)
        sink = jnp.full((per * BLOCK, 1), sink_ref[g * per], F32)
        for hh in range(1, per):
            sink = jnp.where(rowh == hh, sink_ref[g * per + hh], sink)
        m = jnp.maximum(jnp.max(s, axis=-1, keepdims=True), sink)
        p = jnp.exp(s - m)
        den = jnp.sum(p, axis=-1, keepdims=True) + jnp.exp(sink - m)
        o = jnp.dot(p.astype(vrep.dtype), vrep, preferred_element_type=F32) * (1.0 / den)
        og = o[0:BLOCK]
        for hh in range(1, per):
            og = jnp.where(sels[hh], o[hh * BLOCK:(hh + 1) * BLOCK], og)
        o_ref[0, :, g * gw:(g + 1) * gw] = og.astype(o_ref.dtype)


def _swa_attn(proj3, lay, sinks, bias_masked):
    B, S, _ = proj3.shape
    nb = S // BLOCK
    groups, per = SWA_KV_HEADS, SWA_Q_HEADS // SWA_KV_HEADS
    qw = SWA_Q_HEADS * SWA_HEAD_DIM
    kw = SWA_KV_HEADS * SWA_HEAD_DIM
    assert lay["q_s"] % qw == 0 and lay["k_s"] % kw == 0 and lay["v_s"] % kw == 0
    qo, ko, vo = lay["q_s"] // qw, lay["k_s"] // kw, lay["v_s"] // kw
    prev = lambda o: (lambda b, i, s: (b, jnp.maximum(i - 1, 0), o))
    cur = lambda o: (lambda b, i, s: (b, i, o))
    grid_spec = pltpu.PrefetchScalarGridSpec(
        num_scalar_prefetch=1,
        grid=(B, nb),
        in_specs=[pl.BlockSpec((1, BLOCK, qw), cur(qo)),
                  pl.BlockSpec((1, BLOCK, kw), prev(ko)),
                  pl.BlockSpec((1, BLOCK, kw), cur(ko)),
                  pl.BlockSpec((1, BLOCK, kw), prev(vo)),
                  pl.BlockSpec((1, BLOCK, kw), cur(vo)),
                  pl.BlockSpec((1,) + bias_masked.shape[1:], lambda b, i, s: (jnp.minimum(i, 1), 0, 0, 0))],
        out_specs=pl.BlockSpec((1, BLOCK, qw), lambda b, i, s: (b, i, 0)),
    )
    return pl.pallas_call(
        functools.partial(_swa_kernel, groups=groups, per=per),
        out_shape=jax.ShapeDtypeStruct((B, S, qw), BF16),
        grid_spec=grid_spec,
        compiler_params=_cparams(("arbitrary", "arbitrary")),
        name="swa_attn",
    )(sinks, proj3, proj3, proj3, proj3, proj3, bias_masked)


def _merge_kernel(om_ref, os_ref, g0_ref, g1_ref, bg_ref, wm_ref, ws_ref, o_ref):
    ym = jnp.dot(om_ref[...], wm_ref[...], preferred_element_type=F32)
    ys = jnp.dot(os_ref[...], ws_ref[...], preferred_element_type=F32)
    g0 = jax.nn.sigmoid(g0_ref[...].astype(F32) + bg_ref[0:1, :])
    g1 = jax.nn.sigmoid(g1_ref[...].astype(F32) + bg_ref[1:2, :])
    o_ref[...] = (g0 * ym + g1 * ys).astype(o_ref.dtype)


def _merge(om, osw, proj, lay, bg, wm, ws):
    T = om.shape[0]
    D = wm.shape[1]
    tm = _tile(T, MERGE_TM)
    assert lay["g0"] % D == 0 and lay["g1"] % D == 0
    row = lambda w: pl.BlockSpec((tm, w), lambda i: (i, 0))
    full = lambda a: pl.BlockSpec(a.shape, lambda i: (0,) * a.ndim)
    return pl.pallas_call(
        _merge_kernel,
        out_shape=jax.ShapeDtypeStruct((T, D), BF16),
        grid=(T // tm,),
        in_specs=[row(om.shape[1]), row(osw.shape[1]),
                  pl.BlockSpec((tm, D), lambda i, o=lay["g0"] // D: (i, o)),
                  pl.BlockSpec((tm, D), lambda i, o=lay["g1"] // D: (i, o)),
                  full(bg), full(wm), full(ws)],
        out_specs=row(D),
        compiler_params=_cparams(("arbitrary",)),
        name="merge",
    )(om, osw, proj, proj, bg, wm, ws)


def _layer_norm(z, g, b):
    mu = jnp.mean(z, axis=-1, keepdims=True)
    zc = z - mu
    var = jnp.mean(zc * zc, axis=-1, keepdims=True)
    return zc * lax.rsqrt(var + LN_EPS) * g + b


def _pack_pairs(y):
    n = y.shape[1] // 2
    lo = pltpu.bitcast(y[:, :n].astype(BF16).astype(F32), jnp.uint32)
    hi = pltpu.bitcast(y[:, n:].astype(BF16).astype(F32), jnp.uint32)
    return (lo >> 16) | (hi & jnp.uint32(0xFFFF0000))


def _unpack_pairs(w):
    lo = pltpu.bitcast(w << 16, F32)
    hi = pltpu.bitcast(w & jnp.uint32(0xFFFF0000), F32)
    return lo, hi


def _out_ln_kernel(mg_ref, x_ref, wo_ref, g_ref, b_ref, wrt_ref, h_ref, hb_ref, hp_ref, st_ref):
    mix = jnp.dot(mg_ref[...], wo_ref[...], preferred_element_type=F32)
    h = _layer_norm(ALPHA * x_ref[...] + mix, g_ref[...], b_ref[...])
    h_ref[...] = h
    hb = h.astype(BF16)
    hb_ref[...] = hb
    hp_ref[...] = _pack_pairs(h)
    logits_t = lax.dot_general(wrt_ref[...], hb, (((1,), (1,)), ((), ())), preferred_element_type=F32)
    st_ref[...] = jax.nn.sigmoid(logits_t)


def _out_ln(merged, x2, wo, g, b, wrt):
    T, D = x2.shape
    E = wrt.shape[0]
    tm = _tile(T, OUT_TM)
    row = lambda w: pl.BlockSpec((tm, w), lambda i: (i, 0))
    full = lambda a: pl.BlockSpec(a.shape, lambda i: (0,) * a.ndim)
    return pl.pallas_call(
        _out_ln_kernel,
        out_shape=(jax.ShapeDtypeStruct((T, D), F32), jax.ShapeDtypeStruct((T, D), BF16),
                   jax.ShapeDtypeStruct((T, D // 2), jnp.uint32), jax.ShapeDtypeStruct((E, T), F32)),
        grid=(T // tm,),
        in_specs=[row(D), row(D), full(wo), full(g), full(b), full(wrt)],
        out_specs=(row(D), row(D), row(D // 2), pl.BlockSpec((E, tm), lambda i: (0, i))),
        compiler_params=_cparams(("arbitrary",)),
        name="out_ln",
    )(merged, x2, wo, g, b, wrt)


def _router_kernel(s_ref, b_ref, u_ref, te_ref, w_ref, rk_ref, cnt_ref, carry_ref, *, per):
    G = N_GROUPS
    tm = s_ref.shape[1]
    ninf = -jnp.inf

    @pl.when(pl.program_id(0) == 0)
    def _():
        carry_ref[...] = jnp.zeros_like(carry_ref)

    giota = lax.broadcasted_iota(jnp.int32, (G, tm), 0)
    s = [s_ref[j * G:(j + 1) * G, :] for j in range(per)]
    sel = [s[j] + b_ref[j * G:(j + 1) * G, :] for j in range(per)]
    eidx = [giota * per + j for j in range(per)]

    m1 = functools.reduce(jnp.maximum, sel)
    j1 = functools.reduce(jnp.minimum, [jnp.where(sel[j] == m1, j, per) for j in range(per)])
    m2 = functools.reduce(jnp.maximum, [jnp.where(j1 == j, ninf, sel[j]) for j in range(per)])
    grp = m1 + m2
    gmask = jnp.zeros((G, tm), jnp.bool_)
    for _ in range(TOPK_GROUPS):
        mx = jnp.max(grp, axis=0, keepdims=True)
        gi = jnp.min(jnp.where(grp == mx, giota, G), axis=0, keepdims=True)
        chosen = giota == gi
        gmask = jnp.logical_or(gmask, chosen)
        grp = jnp.where(chosen, ninf, grp)

    cur = [jnp.where(gmask, sel[j], ninf) for j in range(per)]
    mem = [jnp.zeros((G, tm), F32) for _ in range(per)]
    e_rows, w_rows = [], []
    for _ in range(TOP_K):
        mx = jnp.max(functools.reduce(jnp.maximum, cur), axis=0, keepdims=True)
        cand = functools.reduce(jnp.minimum, [jnp.where(cur[j] == mx, eidx[j], N_EXPERTS) for j in range(per)])
        emin = jnp.min(cand, axis=0, keepdims=True)
        wk = jnp.zeros((1, tm), F32)
        for j in range(per):
            hit = eidx[j] == emin
            wk = wk + jnp.sum(jnp.where(hit, s[j], 0.0), axis=0, keepdims=True)
            mem[j] = jnp.where(hit, 1.0, mem[j])
            cur[j] = jnp.where(hit, ninf, cur[j])
        e_rows.append(emin)
        w_rows.append(wk)
    wsum = functools.reduce(lambda a, b: a + b, w_rows)

    memall = jnp.concatenate(mem, axis=0)
    pre = jnp.dot(memall.astype(BF16), u_ref[...], preferred_element_type=F32)
    rank = pre + carry_ref[:, 0:1]
    tot = carry_ref[...] + jnp.sum(memall, axis=1, keepdims=True)
    carry_ref[...] = tot
    cnt_ref[...] = tot

    kiota = lax.broadcasted_iota(jnp.int32, (TOP_K, tm), 0)
    te = jnp.zeros((TOP_K, tm), jnp.int32)
    wt = jnp.zeros((TOP_K, tm), F32)
    rk = jnp.zeros((TOP_K, tm), F32)
    for k in range(TOP_K):
        rk_k = jnp.zeros((1, tm), F32)
        for j in range(per):
            rk_k = rk_k + jnp.sum(jnp.where(eidx[j] == e_rows[k], rank[j * G:(j + 1) * G, :], 0.0),
                                  axis=0, keepdims=True)
        te = jnp.where(kiota == k, e_rows[k], te)
        wt = jnp.where(kiota == k, w_rows[k] / wsum * ROUTED_SCALE, wt)
        rk = jnp.where(kiota == k, rk_k, rk)
    te_ref[...] = te
    w_ref[...] = wt
    rk_ref[...] = rk.astype(jnp.int32)


def _router(scores_t, bias_col, utri):
    E, T = scores_t.shape
    tm = utri.shape[0]
    per = E // N_GROUPS
    tk = lambda: pl.BlockSpec((TOP_K, tm), lambda i: (0, i))
    return pl.pallas_call(
        functools.partial(_router_kernel, per=per),
        out_shape=(jax.ShapeDtypeStruct((TOP_K, T), jnp.int32), jax.ShapeDtypeStruct((TOP_K, T), F32),
                   jax.ShapeDtypeStruct((TOP_K, T), jnp.int32), jax.ShapeDtypeStruct((E, LANE), F32)),
        grid=(T // tm,),
        in_specs=[pl.BlockSpec((E, tm), lambda i: (0, i)),
                  pl.BlockSpec((E, 1), lambda i: (0, 0)),
                  pl.BlockSpec((tm, tm), lambda i: (0, 0))],
        out_specs=(tk(), tk(), tk(), pl.BlockSpec((E, LANE), lambda i: (0, 0))),
        scratch_shapes=[pltpu.VMEM((E, LANE), F32)],
        compiler_params=_cparams(("arbitrary",)),
        name="router",
    )(scores_t, bias_col, utri)


def _dispatch_kernel(dest_ref, h_ref, xs_ref, sem):
    tm = h_ref.shape[0]

    def body(r, c):
        for k in range(TOP_K):
            pltpu.make_async_copy(h_ref.at[pl.ds(r, 1)], xs_ref.at[pl.ds(dest_ref[k, r], 1)], sem).start()
        return c

    lax.fori_loop(0, tm, body, 0)
    for k in range(TOP_K):
        pltpu.make_async_copy(h_ref, xs_ref.at[pl.ds(0, tm)], sem).wait()


def _dispatch(dest_t, hp):
    T, W = hp.shape
    tm = _tile(T, DISPATCH_TM)
    return pl.pallas_call(
        _dispatch_kernel,
        out_shape=jax.ShapeDtypeStruct((T * TOP_K, W), hp.dtype),
        grid=(T // tm,),
        in_specs=[pl.BlockSpec((TOP_K, tm), lambda i: (0, i), memory_space=pltpu.SMEM),
                  pl.BlockSpec((tm, W), lambda i: (i, 0))],
        out_specs=pl.BlockSpec(memory_space=pl.ANY),
        scratch_shapes=[pltpu.SemaphoreType.DMA(())],
        compiler_params=_cparams(("arbitrary",)),
        name="dispatch",
    )(dest_t, hp)


def _moe_kernel(ib_ref, ie_ref, lo_ref, hi_ref, n_ref, x_ref, wgu_ref, wd_ref, y_ref, acc_ref,
                wgu_b, wd_b, *, ff):
    i = pl.program_id(0)
    n = n_ref[0]
    last_i = ib_ref.shape[0] - 1
    tm = x_ref.shape[0]
    half = x_ref.shape[1]

    @pl.when(i < n)
    def _():
        b = ib_ref[i]
        first = jnp.logical_or(i == 0, ib_ref[jnp.maximum(i - 1, 0)] != b)
        last = jnp.logical_or(i == n - 1, ib_ref[jnp.minimum(i + 1, last_i)] != b)
        xlo, xhi = _unpack_pairs(x_ref[...])
        @pl.when(jnp.logical_or(i == 0, ie_ref[jnp.maximum(i - 1, 0)] != ie_ref[i]))
        def _():
            wgu_b[...] = wgu_ref[0].astype(BF16)
            wd_b[...] = wd_ref[0].astype(BF16)

        xb = jnp.concatenate([xlo.astype(BF16), xhi.astype(BF16)], axis=1)
        h = jnp.dot(xb, wgu_b[...], preferred_element_type=F32)
        a = (jax.nn.silu(h[:, :ff]) * h[:, ff:]).astype(BF16)
        y = jnp.dot(a, wd_b[...], preferred_element_type=F32)
        rows = lax.broadcasted_iota(jnp.int32, (tm, 1), 0)
        keep = jnp.logical_and(rows >= lo_ref[i], rows < hi_ref[i])
        y = jnp.where(keep, y, 0.0)

        @pl.when(jnp.logical_and(first, last))
        def _():
            y_ref[...] = _pack_pairs(y)

        @pl.when(jnp.logical_and(first, jnp.logical_not(last)))
        def _():
            acc_ref[...] = y

        @pl.when(jnp.logical_not(first))
        def _():
            acc_ref[...] += y

        @pl.when(jnp.logical_and(last, jnp.logical_not(first)))
        def _():
            y_ref[...] = _pack_pairs(acc_ref[...])


def _moe_experts(ib, ie, lo, hi, n_items, xs, wgu, wd, tm):
    N, W = xs.shape
    ff = wd.shape[1]
    D = wd.shape[2]
    grid_spec = pltpu.PrefetchScalarGridSpec(
        num_scalar_prefetch=5,
        grid=(ib.shape[0],),
        in_specs=[pl.BlockSpec((tm, W), lambda i, ib, ie, lo, hi, n: (ib[i], 0)),
                  pl.BlockSpec((1, D, 2 * ff), lambda i, ib, ie, lo, hi, n: (ie[i], 0, 0)),
                  pl.BlockSpec((1, ff, D), lambda i, ib, ie, lo, hi, n: (ie[i], 0, 0))],
        out_specs=pl.BlockSpec((tm, W), lambda i, ib, ie, lo, hi, n: (ib[i], 0)),
        scratch_shapes=[pltpu.VMEM((tm, D), F32), pltpu.VMEM((D, 2 * ff), BF16), pltpu.VMEM((ff, D), BF16)],
    )
    return pl.pallas_call(
        functools.partial(_moe_kernel, ff=ff),
        out_shape=jax.ShapeDtypeStruct((N, W), jnp.uint32),
        grid_spec=grid_spec,
        compiler_params=_cparams(("arbitrary",)),
        name="moe_experts",
    )(ib, ie, lo, hi, n_items, xs, wgu, wd)


def _combine_kernel(dest_ref, dnext_ref, ys_ref, w_ref, h_ref, hb_ref, wsg_ref, wsd_ref, g_ref, b_ref, o_ref,
                    ybuf, routed, sems, *, ff):
    i = pl.program_id(0)
    last = pl.num_programs(0) - 1
    tm = h_ref.shape[0]
    half = ybuf.shape[3]
    sub = 8
    slot = lax.rem(i, 2)
    nslot = 1 - slot

    def issue(idx_ref, r, to_slot):
        for k in range(TOP_K):
            pltpu.make_async_copy(ys_ref.at[pl.ds(idx_ref[k, r], 1)], ybuf.at[to_slot, k, pl.ds(r, 1)],
                                  sems.at[to_slot]).start()

    @pl.when(i == 0)
    def _():
        def first(r, c):
            issue(dest_ref, r, slot)
            return c
        lax.fori_loop(0, tm, first, 0)

    for k in range(TOP_K):
        pltpu.make_async_copy(ys_ref.at[pl.ds(0, tm)], ybuf.at[slot, k], sems.at[slot]).wait()

    def sum_rows(r0):
        w = w_ref[pl.ds(r0, sub), :]
        rlo = jnp.zeros((sub, half), F32)
        rhi = jnp.zeros((sub, half), F32)
        for k in range(TOP_K):
            lo, hi = _unpack_pairs(ybuf[slot, k, pl.ds(r0, sub), :])
            rlo = rlo + lo * w[:, k:k + 1]
            rhi = rhi + hi * w[:, k:k + 1]
        routed[pl.ds(r0, sub), :half] = rlo
        routed[pl.ds(r0, sub), half:] = rhi

    @pl.when(i < last)
    def _():
        def fused(t, c):
            r0 = pl.multiple_of(t * sub, sub)
            sum_rows(r0)
            for j in range(sub):
                issue(dnext_ref, r0 + j, nslot)
            return c
        lax.fori_loop(0, tm // sub, fused, 0)

    @pl.when(i == last)
    def _():
        def tail(t, c):
            sum_rows(pl.multiple_of(t * sub, sub))
            return c
        lax.fori_loop(0, tm // sub, tail, 0)

    s = jnp.dot(hb_ref[...], wsg_ref[...], preferred_element_type=F32)
    a = (jax.nn.silu(s[:, :ff]) * s[:, ff:]).astype(BF16)
    shared = jnp.dot(a, wsd_ref[...], preferred_element_type=F32)
    o_ref[...] = _layer_norm(ALPHA * h_ref[...] + routed[...] + shared, g_ref[...], b_ref[...])


def _moe_combine(dest_t, ys, w, h, hb, wsg, wsd, g, b):
    T, D = h.shape
    ff = wsd.shape[0]
    tm = _tile(T, COMBINE_TM)
    nt = T // tm
    row = lambda wd_: pl.BlockSpec((tm, wd_), lambda i: (i, 0))
    full = lambda a: pl.BlockSpec(a.shape, lambda i: (0,) * a.ndim)
    return pl.pallas_call(
        functools.partial(_combine_kernel, ff=ff),
        out_shape=jax.ShapeDtypeStruct((T, D), F32),
        grid=(T // tm,),
        in_specs=[pl.BlockSpec((TOP_K, tm), lambda i: (0, i), memory_space=pltpu.SMEM),
                  pl.BlockSpec((TOP_K, tm), lambda i: (0, jnp.minimum(i + 1, nt - 1)), memory_space=pltpu.SMEM),
                  pl.BlockSpec(memory_space=pl.ANY), row(TOP_K), row(D), row(D),
                  full(wsg), full(wsd), full(g), full(b)],
        out_specs=row(D),
        scratch_shapes=[pltpu.VMEM((2, TOP_K, tm, ys.shape[1]), ys.dtype), pltpu.VMEM((tm, D), F32),
                        pltpu.SemaphoreType.DMA((2,))],
        compiler_params=_cparams(("arbitrary",)),
        name="moe_combine",
    )(dest_t, dest_t, ys, w, h, hb, wsg, wsd, g, b)


def _rot_cols(w):
    half = w.shape[-1] // 2
    return jnp.concatenate([-w[..., half:], w[..., :half]], axis=-1)


def _t5_bucket(dist):
    n = jnp.maximum(dist, 0)
    max_exact = REL_BUCKETS // 2
    large = max_exact + (jnp.log(jnp.maximum(n, 1).astype(F32) / max_exact)
                         / math.log(REL_MAX_DIST / max_exact) * (REL_BUCKETS - max_exact)).astype(jnp.int32)
    large = jnp.minimum(large, REL_BUCKETS - 1)
    return jnp.where(n < max_exact, n, large)


def _layout(widths):
    order = sorted(range(len(widths)), key=lambda j: -widths[j][1])
    lay, off = {}, 0
    for j in order:
        name, w = widths[j]
        assert off % w == 0, (name, off, w)
        lay[name] = off
        off += w
    return lay, off, [widths[j][0] for j in order]


def _work_items(gs, tm, n_blocks):
    E = gs.shape[0]
    end = jnp.cumsum(gs)
    start = end - gs
    b0 = start // tm
    n_e = jnp.where(gs > 0, (end + tm - 1) // tm - b0, 0)
    icum = jnp.cumsum(n_e)
    n_items = icum[-1]
    cap = n_blocks + E
    idx = jnp.arange(cap, dtype=jnp.int32)
    ie = jnp.minimum(jnp.sum(idx[:, None] >= icum[None, :], axis=1), E - 1).astype(jnp.int32)
    ib = (b0[ie] + idx - (icum - n_e)[ie]).astype(jnp.int32)
    lo = jnp.maximum(start[ie] - ib * tm, 0)
    hi = jnp.minimum(end[ie] - ib * tm, tm)
    ok = idx < n_items
    ib = jnp.where(ok, ib, n_blocks - 1).astype(jnp.int32)
    lo = jnp.where(ok, lo, 0).astype(jnp.int32)
    hi = jnp.where(ok, hi, 0).astype(jnp.int32)
    return start, ib, ie, lo, hi, n_items.astype(jnp.int32).reshape(1)


def kernel(x, positions, w_in, b_gate, q_norm_g, kv_norm_g, w_uq, w_uk, w_uv, swa_sinks, rel_table,
           w_br_mla, w_br_swa, w_out, ln1_g, ln1_b, w_router, router_bias, w_gate_up, w_down,
           w_shared_gate_up, w_shared_down, ln2_g, ln2_b):
    B, S, D = x.shape
    T = B * S
    H = MLA_HEADS
    E = N_EXPERTS
    assert w_in.shape[0] == DEPTH == 1
    qr, kvr = w_uq.shape[1], w_uk.shape[1]
    rd = MLA_ROPE_DIM
    qw = SWA_Q_HEADS * SWA_HEAD_DIM
    kw = SWA_KV_HEADS * SWA_HEAD_DIM

    sizes = (qr, kvr, rd, qw, kw, kw, N_BRANCHES * D)
    bounds = np.cumsum(sizes)[:-1].tolist()
    w_cq, w_ckv, w_kr, w_qs, w_ks, w_vs, w_gate = jnp.split(w_in[0], bounds, axis=-1)
    zpad = jnp.zeros((D, LANE - rd), F32)
    segs = {"g0": w_gate[:, :D], "g1": w_gate[:, D:], "q_s": w_qs * (SWA_HEAD_DIM ** -0.5), "c_q": w_cq, "c_kv": w_ckv,
            "k_s": w_ks, "v_s": w_vs,
            "kra": jnp.concatenate([w_kr, zpad], axis=1),
            "krb": jnp.concatenate([_rot_cols(w_kr), zpad], axis=1)}
    lay, total, order = _layout([(n, int(a.shape[1])) for n, a in segs.items()])
    w_p = jnp.concatenate([segs[n] for n in order], axis=1).astype(BF16)

    wq = w_uq[0].reshape(qr, H, MLA_NOPE_DIM + rd)
    wq_nope, wq_rope = wq[..., :MLA_NOPE_DIM], wq[..., MLA_NOPE_DIM:]
    z64 = jnp.zeros((qr, H, LANE - rd), F32)
    wqa = jnp.concatenate([wq_nope, wq_rope, z64], axis=-1).reshape(qr, H * 2 * LANE).astype(BF16)
    wqb = jnp.concatenate([_rot_cols(wq_rope), z64], axis=-1).reshape(qr, H * LANE).astype(BF16)
    wuk = w_uk[0].astype(BF16)
    wuv = w_uv[0].astype(BF16)

    inv = ROPE_THETA ** (-jnp.arange(0, rd, 2, dtype=F32) / rd)
    ang = positions.astype(F32).reshape(T, 1) * inv
    zc = jnp.zeros((T, LANE - rd), F32)
    cosp = jnp.concatenate([jnp.cos(ang), jnp.cos(ang), zc], axis=1)
    sinp = jnp.concatenate([jnp.sin(ang), jnp.sin(ang), zc], axis=1)

    qi = jnp.arange(BLOCK)[:, None]
    kj = jnp.arange(2 * BLOCK)[None, :]
    dist = qi + BLOCK - kj
    onehot = (_t5_bucket(dist)[None] == jnp.arange(REL_BUCKETS)[:, None, None]).astype(F32)
    bias = jnp.einsum("rh,rqk->hqk", rel_table.astype(F32), onehot, precision=lax.Precision.HIGHEST)
    in_window = (dist >= 0) & (dist < WINDOW)
    bias_masked = jnp.stack([jnp.where((in_window & (kj >= BLOCK))[None], bias, NEG_BIG),
                             jnp.where(in_window[None], bias, NEG_BIG)])

    per = E // N_GROUPS
    perm = np.arange(E).reshape(N_GROUPS, per).T.reshape(-1)
    wrt = w_router[0].T[perm].astype(BF16)
    bias_col = router_bias[0].astype(F32)[perm].reshape(E, 1)
    rt = _tile(T, ROUTER_TM)
    utri = jnp.asarray(np.triu(np.ones((rt, rt), np.float32), k=1), dtype=BF16)

    x2 = x.reshape(T, D)

    proj = _in_proj(x2, w_p)
    q, k, v = _mla_proj(proj, lay, cosp, sinp, q_norm_g[0].reshape(1, qr), kv_norm_g[0].reshape(1, kvr),
                        wqa, wqb, wuk, wuv, H)
    o_m = _mla_attn(q.reshape(B, S, -1), k.reshape(B, S, -1), v.reshape(B, S, -1), H).reshape(T, -1)
    o_s = _swa_attn(proj.reshape(B, S, total), lay, swa_sinks[0].astype(F32), bias_masked).reshape(T, qw)
    merged = _merge(o_m, o_s, proj, lay, b_gate[0], w_br_mla[0].astype(BF16), w_br_swa[0].astype(BF16))
    h, hb, hp, scores_t = _out_ln(merged, x2, w_out[0].astype(BF16), ln1_g[0].reshape(1, D),
                                  ln1_b[0].reshape(1, D), wrt)

    te_t, w_t, rk_t, cnt = _router(scores_t, bias_col, utri)
    inv_perm = np.argsort(perm)
    gs = cnt[:, 0].astype(jnp.int32)[inv_perm]
    tm = MOE_TM
    N = T * TOP_K
    assert N % tm == 0
    start, ib, ie, lo, hi, n_items = _work_items(gs, tm, N // tm)
    dest_t = rk_t + jnp.sum(jnp.where(te_t[:, :, None] == jnp.arange(E, dtype=jnp.int32),
                                      start.astype(jnp.int32), 0), axis=-1)

    xs = _dispatch(dest_t, hp)
    ys = _moe_experts(ib, ie, lo, hi, n_items, xs, w_gate_up[0], w_down[0], tm)
    out = _moe_combine(dest_t, ys, w_t.T, h, hb, w_shared_gate_up[0].astype(BF16),
                       w_shared_down[0].astype(BF16), ln2_g[0].reshape(1, D), ln2_b[0].reshape(1, D))
    return out.reshape(B, S, D)
```

```python
import functools
import math

import jax
import jax.numpy as jnp
import numpy as np
from jax import lax
from jax.experimental import pallas as pl
from jax.experimental.pallas import tpu as pltpu

F32 = jnp.float32
BF16 = jnp.bfloat16

MLA_HEADS = 8
MLA_NOPE_DIM = 128
MLA_ROPE_DIM = 64
MLA_V_DIM = 128
ROPE_THETA = 10000.0
SWA_Q_HEADS = 16
SWA_KV_HEADS = 4
SWA_HEAD_DIM = 64
WINDOW = 128
BLOCK = 128
REL_BUCKETS = 32
REL_MAX_DIST = 128
N_BRANCHES = 2
N_EXPERTS = 64
TOP_K = 8
N_GROUPS = 8
TOPK_GROUPS = 4
ROUTED_SCALE = 2.5
DEPTH = 1
ALPHA = (2 * DEPTH) ** 0.25
LN_EPS = 1e-5
RMS_EPS = 1e-6

LANE = 128
NEG_BIG = -1e30

PROJ_TM = 512
MLA_PROJ_TM = 512
MLA_TQ = 512
MERGE_TM = 512
OUT_TM = 512
MOE_TM = 512
COMBINE_TM = 256
ROUTER_TM = 512
DISPATCH_TM = 512
VMEM_LIMIT = 56 * 1024 * 1024


def _cparams(sem):
    return pltpu.CompilerParams(dimension_semantics=sem, vmem_limit_bytes=VMEM_LIMIT)


def _tile(n, t):
    t = min(n, t)
    assert n % t == 0, (n, t)
    return t


def _in_proj_kernel(x_ref, w_ref, o_ref):
    x = x_ref[...].astype(BF16)
    o_ref[...] = jnp.dot(x, w_ref[...], preferred_element_type=F32).astype(o_ref.dtype)


def _in_proj(x2, w_p):
    T, D = x2.shape
    N = w_p.shape[1]
    tn = N // 2 if (N // 2) % LANE == 0 and N % 2 == 0 else N
    tm = _tile(T, PROJ_TM)
    return pl.pallas_call(
        _in_proj_kernel,
        out_shape=jax.ShapeDtypeStruct((T, N), BF16),
        grid=(N // tn, T // tm),
        in_specs=[pl.BlockSpec((tm, D), lambda j, i: (i, 0)),
                  pl.BlockSpec((D, tn), lambda j, i: (0, j))],
        out_specs=pl.BlockSpec((tm, tn), lambda j, i: (i, j)),
        compiler_params=_cparams(("arbitrary", "arbitrary")),
        name="in_proj",
    )(x2, w_p)


def _mla_proj_kernel(cq_ref, ckv_ref, kra_ref, krb_ref, cos_ref, sin_ref, qg_ref, kvg_ref,
                     wqa_ref, wqb_ref, wuk_ref, wuv_ref, q_ref, k_ref, v_ref, *, heads, scale):
    cos = cos_ref[...]
    sin = sin_ref[...]
    cq = cq_ref[...].astype(F32)
    qn = cq * lax.rsqrt(jnp.mean(cq * cq, axis=-1, keepdims=True) + RMS_EPS) * qg_ref[...]
    qn = qn.astype(BF16)
    qa = jnp.dot(qn, wqa_ref[...], preferred_element_type=F32)
    qb = jnp.dot(qn, wqb_ref[...], preferred_element_type=F32)
    ckv = ckv_ref[...].astype(F32)
    cn = ckv * lax.rsqrt(jnp.mean(ckv * ckv, axis=-1, keepdims=True) + RMS_EPS) * kvg_ref[...]
    cn = cn.astype(BF16)
    kn = jnp.dot(cn, wuk_ref[...], preferred_element_type=F32)
    v_ref[...] = jnp.dot(cn, wuv_ref[...], preferred_element_type=F32).astype(v_ref.dtype)
    krope = (kra_ref[...].astype(F32) * cos + krb_ref[...].astype(F32) * sin).astype(k_ref.dtype)
    for h in range(heads):
        lo = h * 2 * LANE
        q_ref[:, lo:lo + LANE] = (qa[:, lo:lo + LANE] * scale).astype(q_ref.dtype)
        q_ref[:, lo + LANE:lo + 2 * LANE] = (
            (qa[:, lo + LANE:lo + 2 * LANE] * cos + qb[:, h * LANE:(h + 1) * LANE] * sin) * scale
        ).astype(q_ref.dtype)
        k_ref[:, lo:lo + LANE] = kn[:, h * LANE:(h + 1) * LANE].astype(k_ref.dtype)
        k_ref[:, lo + LANE:lo + 2 * LANE] = krope


def _mla_proj(proj, lay, cosp, sinp, qg, kvg, wqa, wqb, wuk, wuv, heads):
    T = proj.shape[0]
    tm = _tile(T, MLA_PROJ_TM)
    qr, kvr = wqa.shape[0], wuk.shape[0]

    def col(name, width):
        off = lay[name]
        assert off % width == 0
        return pl.BlockSpec((tm, width), lambda i, o=off // width: (i, o))

    row = lambda w: pl.BlockSpec((tm, w), lambda i: (i, 0))
    full = lambda a: pl.BlockSpec(a.shape, lambda i: (0,) * a.ndim)
    scale = (MLA_NOPE_DIM + MLA_ROPE_DIM) ** -0.5
    return pl.pallas_call(
        functools.partial(_mla_proj_kernel, heads=heads, scale=scale),
        out_shape=(jax.ShapeDtypeStruct((T, heads * 2 * LANE), BF16),
                   jax.ShapeDtypeStruct((T, heads * 2 * LANE), BF16),
                   jax.ShapeDtypeStruct((T, heads * MLA_V_DIM), BF16)),
        grid=(T // tm,),
        in_specs=[col("c_q", qr), col("c_kv", kvr), col("kra", LANE), col("krb", LANE),
                  row(LANE), row(LANE), full(qg), full(kvg), full(wqa), full(wqb), full(wuk), full(wuv)],
        out_specs=(row(heads * 2 * LANE), row(heads * 2 * LANE), row(heads * MLA_V_DIM)),
        compiler_params=_cparams(("arbitrary",)),
        name="mla_proj",
    )(proj, proj, proj, proj, cosp, sinp, qg, kvg, wqa, wqb, wuk, wuv)


def _mla_attn_kernel(q_ref, k_ref, v_ref, o_ref, *, tq, hp, dv):
    i = pl.program_id(2)
    dk = 2 * LANE
    dn = (((1,), (1,)), ((), ()))
    qs = [q_ref[0, :, hh * dk:(hh + 1) * dk] for hh in range(hp)]

    def step(q, kb, vb, carry, mask):
        m, l, acc = carry
        s = lax.dot_general(q, kb, dn, preferred_element_type=F32)
        if mask is not None:
            s = jnp.where(mask, s, NEG_BIG)
        m_new = jnp.maximum(m, jnp.max(s, axis=-1, keepdims=True))
        p = jnp.exp(s - m_new)
        a = jnp.exp(m - m_new)
        l = a * l + jnp.sum(p, axis=-1, keepdims=True)
        acc = a * acc + jnp.dot(p.astype(vb.dtype), vb, preferred_element_type=F32)
        return m_new, l, acc

    def block(off, carries, mask):
        return tuple(step(qs[hh], k_ref[0, pl.ds(off, tq), hh * dk:(hh + 1) * dk],
                          v_ref[0, pl.ds(off, tq), hh * dv:(hh + 1) * dv], carries[hh], mask)
                     for hh in range(hp))

    def body(j, carries):
        return block(pl.multiple_of(j * tq, tq), carries, None)

    init = tuple((jnp.full((tq, 1), NEG_BIG, F32), jnp.zeros((tq, 1), F32), jnp.zeros((tq, dv), F32))
                 for _ in range(hp))
    carries = lax.fori_loop(0, i, body, init)
    r = lax.broadcasted_iota(jnp.int32, (tq, tq), 0)
    c = lax.broadcasted_iota(jnp.int32, (tq, tq), 1)
    carries = block(pl.multiple_of(i * tq, tq), carries, c <= r)
    for hh in range(hp):
        m, l, acc = carries[hh]
        o_ref[0, :, hh * dv:(hh + 1) * dv] = (acc / l).astype(o_ref.dtype)


def _mla_attn(q, k, v, heads):
    B, S, _ = q.shape
    tq = _tile(S, MLA_TQ)
    dv = v.shape[-1] // heads
    hp = 2 if heads % 2 == 0 else 1
    return pl.pallas_call(
        functools.partial(_mla_attn_kernel, tq=tq, hp=hp, dv=dv),
        out_shape=jax.ShapeDtypeStruct((B, S, heads * dv), BF16),
        grid=(B, heads // hp, S // tq),
        in_specs=[pl.BlockSpec((1, tq, hp * 2 * LANE), lambda b, h, i: (b, i, h)),
                  pl.BlockSpec((1, S, hp * 2 * LANE), lambda b, h, i: (b, 0, h)),
                  pl.BlockSpec((1, S, hp * dv), lambda b, h, i: (b, 0, h))],
        out_specs=pl.BlockSpec((1, tq, hp * dv), lambda b, h, i: (b, i, h)),
        compiler_params=_cparams(("arbitrary", "arbitrary", "arbitrary")),
        name="mla_attn",
    )(q, k, v)


def _swa_kernel(sink_ref, q_ref, kp_ref, kc_ref, vp_ref, vc_ref, bias_ref, o_ref, *, groups, per):
    hd = SWA_HEAD_DIM
    gw = per * hd
    kband = jnp.concatenate([kp_ref[0], kc_ref[0]], axis=0)
    vband = jnp.concatenate([vp_ref[0], vc_ref[0]], axis=0)
    lane = lax.broadcasted_iota(jnp.int32, (BLOCK, gw), 1)
    sels = [jnp.logical_and(lane >= hh * hd, lane < (hh + 1) * hd) for hh in range(per)]
    rowh = lax.broadcasted_iota(jnp.int32, (per * BLOCK, 1), 0) // BLOCK
    dn = (((1,), (1,)), ((), ()))
    for g in range(groups):
        kg = kband[:, g * hd:(g + 1) * hd]
        vg = vband[:, g * hd:(g + 1) * hd]
        krep = jnp.concatenate([kg] * per, axis=1)
        vrep = jnp.concatenate([vg] * per, axis=1)
        qg = q_ref[0, :, g * gw:(g + 1) * gw]
        qs = jnp.concatenate([jnp.where(sels[hh], qg, jnp.zeros_like(qg)) for hh in range(per)], axis=0)
        s = lax.dot_general(qs, krep, dn, preferred_element_type=F32)
        s = s + bias_ref[0, g * per:(g + 1) * per].reshape(per * BLOCK, 2 * BLOCK)
        sink = jnp.full((per * BLOCK, 1), sink_ref[g * per], F32)
        for hh in range(1, per):
            sink = jnp.where(rowh == hh, sink_ref[g * per + hh], sink)
        m = jnp.maximum(jnp.max(s, axis=-1, keepdims=True), sink)
        p = jnp.exp(s - m)
        den = jnp.sum(p, axis=-1, keepdims=True) + jnp.exp(sink - m)
        o = jnp.dot(p.astype(vrep.dtype), vrep, preferred_element_type=F32) * (1.0 / den)
        og = o[0:BLOCK]
        for hh in range(1, per):
            og = jnp.where(sels[hh], o[hh * BLOCK:(hh + 1) * BLOCK], og)
        o_ref[0, :, g * gw:(g + 1) * gw] = og.astype(o_ref.dtype)


def _swa_attn(proj3, lay, sinks, bias_masked):
    B, S, _ = proj3.shape
    nb = S // BLOCK
    groups, per = SWA_KV_HEADS, SWA_Q_HEADS // SWA_KV_HEADS
    qw = SWA_Q_HEADS * SWA_HEAD_DIM
    kw = SWA_KV_HEADS * SWA_HEAD_DIM
    assert lay["q_s"] % qw == 0 and lay["k_s"] % kw == 0 and lay["v_s"] % kw == 0
    qo, ko, vo = lay["q_s"] // qw, lay["k_s"] // kw, lay["v_s"] // kw
    prev = lambda o: (lambda b, i, s: (b, jnp.maximum(i - 1, 0), o))
    cur = lambda o: (lambda b, i, s: (b, i, o))
    grid_spec = pltpu.PrefetchScalarGridSpec(
        num_scalar_prefetch=1,
        grid=(B, nb),
        in_specs=[pl.BlockSpec((1, BLOCK, qw), cur(qo)),
                  pl.BlockSpec((1, BLOCK, kw), prev(ko)),
                  pl.BlockSpec((1, BLOCK, kw), cur(ko)),
                  pl.BlockSpec((1, BLOCK, kw), prev(vo)),
                  pl.BlockSpec((1, BLOCK, kw), cur(vo)),
                  pl.BlockSpec((1,) + bias_masked.shape[1:], lambda b, i, s: (jnp.minimum(i, 1), 0, 0, 0))],
        out_specs=pl.BlockSpec((1, BLOCK, qw), lambda b, i, s: (b, i, 0)),
    )
    return pl.pallas_call(
        functools.partial(_swa_kernel, groups=groups, per=per),
        out_shape=jax.ShapeDtypeStruct((B, S, qw), BF16),
        grid_spec=grid_spec,
        compiler_params=_cparams(("arbitrary", "arbitrary")),
        name="swa_attn",
    )(sinks, proj3, proj3, proj3, proj3, proj3, bias_masked)


def _merge_kernel(om_ref, os_ref, g0_ref, g1_ref, bg_ref, wm_ref, ws_ref, o_ref):
    ym = jnp.dot(om_ref[...], wm_ref[...], preferred_element_type=F32)
    ys = jnp.dot(os_ref[...], ws_ref[...], preferred_element_type=F32)
    g0 = jax.nn.sigmoid(g0_ref[...].astype(F32) + bg_ref[0:1, :])
    g1 = jax.nn.sigmoid(g1_ref[...].astype(F32) + bg_ref[1:2, :])
    o_ref[...] = (g0 * ym + g1 * ys).astype(o_ref.dtype)


def _merge(om, osw, proj, lay, bg, wm, ws):
    T = om.shape[0]
    D = wm.shape[1]
    tm = _tile(T, MERGE_TM)
    assert lay["g0"] % D == 0 and lay["g1"] % D == 0
    row = lambda w: pl.BlockSpec((tm, w), lambda i: (i, 0))
    full = lambda a: pl.BlockSpec(a.shape, lambda i: (0,) * a.ndim)
    return pl.pallas_call(
        _merge_kernel,
        out_shape=jax.ShapeDtypeStruct((T, D), BF16),
        grid=(T // tm,),
        in_specs=[row(om.shape[1]), row(osw.shape[1]),
                  pl.BlockSpec((tm, D), lambda i, o=lay["g0"] // D: (i, o)),
                  pl.BlockSpec((tm, D), lambda i, o=lay["g1"] // D: (i, o)),
                  full(bg), full(wm), full(ws)],
        out_specs=row(D),
        compiler_params=_cparams(("arbitrary",)),
        name="merge",
    )(om, osw, proj, proj, bg, wm, ws)


def _layer_norm(z, g, b):
    mu = jnp.mean(z, axis=-1, keepdims=True)
    zc = z - mu
    var = jnp.mean(zc * zc, axis=-1, keepdims=True)
    return zc * lax.rsqrt(var + LN_EPS) * g + b


def _pack_pairs(y):
    n = y.shape[1] // 2
    lo = pltpu.bitcast(y[:, :n].astype(BF16).astype(F32), jnp.uint32)
    hi = pltpu.bitcast(y[:, n:].astype(BF16).astype(F32), jnp.uint32)
    return (lo >> 16) | (hi & jnp.uint32(0xFFFF0000))


def _unpack_pairs(w):
    lo = pltpu.bitcast(w << 16, F32)
    hi = pltpu.bitcast(w & jnp.uint32(0xFFFF0000), F32)
    return lo, hi


def _out_ln_kernel(mg_ref, x_ref, wo_ref, g_ref, b_ref, wrt_ref, h_ref, hb_ref, hp_ref, st_ref):
    mix = jnp.dot(mg_ref[...], wo_ref[...], preferred_element_type=F32)
    h = _layer_norm(ALPHA * x_ref[...] + mix, g_ref[...], b_ref[...])
    h_ref[...] = h
    hb = h.astype(BF16)
    hb_ref[...] = hb
    hp_ref[...] = _pack_pairs(h)
    logits_t = lax.dot_general(wrt_ref[...], hb, (((1,), (1,)), ((), ())), preferred_element_type=F32)
    st_ref[...] = jax.nn.sigmoid(logits_t)


def _out_ln(merged, x2, wo, g, b, wrt):
    T, D = x2.shape
    E = wrt.shape[0]
    tm = _tile(T, OUT_TM)
    row = lambda w: pl.BlockSpec((tm, w), lambda i: (i, 0))
    full = lambda a: pl.BlockSpec(a.shape, lambda i: (0,) * a.ndim)
    return pl.pallas_call(
        _out_ln_kernel,
        out_shape=(jax.ShapeDtypeStruct((T, D), F32), jax.ShapeDtypeStruct((T, D), BF16),
                   jax.ShapeDtypeStruct((T, D // 2), jnp.uint32), jax.ShapeDtypeStruct((E, T), F32)),
        grid=(T // tm,),
        in_specs=[row(D), row(D), full(wo), full(g), full(b), full(wrt)],
        out_specs=(row(D), row(D), row(D // 2), pl.BlockSpec((E, tm), lambda i: (0, i))),
        compiler_params=_cparams(("arbitrary",)),
        name="out_ln",
    )(merged, x2, wo, g, b, wrt)


def _router_kernel(s_ref, b_ref, u_ref, te_ref, w_ref, rk_ref, cnt_ref, carry_ref, *, per):
    G = N_GROUPS
    tm = s_ref.shape[1]
    ninf = -jnp.inf

    @pl.when(pl.program_id(0) == 0)
    def _():
        carry_ref[...] = jnp.zeros_like(carry_ref)

    giota = lax.broadcasted_iota(jnp.int32, (G, tm), 0)
    s = [s_ref[j * G:(j + 1) * G, :] for j in range(per)]
    sel = [s[j] + b_ref[j * G:(j + 1) * G, :] for j in range(per)]
    eidx = [giota * per + j for j in range(per)]

    m1 = functools.reduce(jnp.maximum, sel)
    j1 = functools.reduce(jnp.minimum, [jnp.where(sel[j] == m1, j, per) for j in range(per)])
    m2 = functools.reduce(jnp.maximum, [jnp.where(j1 == j, ninf, sel[j]) for j in range(per)])
    grp = m1 + m2
    gmask = jnp.zeros((G, tm), jnp.bool_)
    for _ in range(TOPK_GROUPS):
        mx = jnp.max(grp, axis=0, keepdims=True)
        gi = jnp.min(jnp.where(grp == mx, giota, G), axis=0, keepdims=True)
        chosen = giota == gi
        gmask = jnp.logical_or(gmask, chosen)
        grp = jnp.where(chosen, ninf, grp)

    cur = [jnp.where(gmask, sel[j], ninf) for j in range(per)]
    mem = [jnp.zeros((G, tm), F32) for _ in range(per)]
    e_rows, w_rows = [], []
    for _ in range(TOP_K):
        mx = jnp.max(functools.reduce(jnp.maximum, cur), axis=0, keepdims=True)
        cand = functools.reduce(jnp.minimum, [jnp.where(cur[j] == mx, eidx[j], N_EXPERTS) for j in range(per)])
        emin = jnp.min(cand, axis=0, keepdims=True)
        wk = jnp.zeros((1, tm), F32)
        for j in range(per):
            hit = eidx[j] == emin
            wk = wk + jnp.sum(jnp.where(hit, s[j], 0.0), axis=0, keepdims=True)
            mem[j] = jnp.where(hit, 1.0, mem[j])
            cur[j] = jnp.where(hit, ninf, cur[j])
        e_rows.append(emin)
        w_rows.append(wk)
    wsum = functools.reduce(lambda a, b: a + b, w_rows)

    memall = jnp.concatenate(mem, axis=0)
    pre = jnp.dot(memall.astype(BF16), u_ref[...], preferred_element_type=F32)
    rank = pre + carry_ref[:, 0:1]
    tot = carry_ref[...] + jnp.sum(memall, axis=1, keepdims=True)
    carry_ref[...] = tot
    cnt_ref[...] = tot

    kiota = lax.broadcasted_iota(jnp.int32, (TOP_K, tm), 0)
    te = jnp.zeros((TOP_K, tm), jnp.int32)
    wt = jnp.zeros((TOP_K, tm), F32)
    rk = jnp.zeros((TOP_K, tm), F32)
    for k in range(TOP_K):
        rk_k = jnp.zeros((1, tm), F32)
        for j in range(per):
            rk_k = rk_k + jnp.sum(jnp.where(eidx[j] == e_rows[k], rank[j * G:(j + 1) * G, :], 0.0),
                                  axis=0, keepdims=True)
        te = jnp.where(kiota == k, e_rows[k], te)
        wt = jnp.where(kiota == k, w_rows[k] / wsum * ROUTED_SCALE, wt)
        rk = jnp.where(kiota == k, rk_k, rk)
    te_ref[...] = te
    w_ref[...] = wt
    rk_ref[...] = rk.astype(jnp.int32)


def _router(scores_t, bias_col, utri):
    E, T = scores_t.shape
    tm = utri.shape[0]
    per = E // N_GROUPS
    tk = lambda: pl.BlockSpec((TOP_K, tm), lambda i: (0, i))
    return pl.pallas_call(
        functools.partial(_router_kernel, per=per),
        out_shape=(jax.ShapeDtypeStruct((TOP_K, T), jnp.int32), jax.ShapeDtypeStruct((TOP_K, T), F32),
                   jax.ShapeDtypeStruct((TOP_K, T), jnp.int32), jax.ShapeDtypeStruct((E, LANE), F32)),
        grid=(T // tm,),
        in_specs=[pl.BlockSpec((E, tm), lambda i: (0, i)),
                  pl.BlockSpec((E, 1), lambda i: (0, 0)),
                  pl.BlockSpec((tm, tm), lambda i: (0, 0))],
        out_specs=(tk(), tk(), tk(), pl.BlockSpec((E, LANE), lambda i: (0, 0))),
        scratch_shapes=[pltpu.VMEM((E, LANE), F32)],
        compiler_params=_cparams(("arbitrary",)),
        name="router",
    )(scores_t, bias_col, utri)


def _dispatch_kernel(dest_ref, h_ref, xs_ref, sem):
    tm = h_ref.shape[0]

    def body(r, c):
        for k in range(TOP_K):
            pltpu.make_async_copy(h_ref.at[pl.ds(r, 1)], xs_ref.at[pl.ds(dest_ref[k, r], 1)], sem).start()
        return c

    lax.fori_loop(0, tm, body, 0)
    for k in range(TOP_K):
        pltpu.make_async_copy(h_ref, xs_ref.at[pl.ds(0, tm)], sem).wait()


def _dispatch(dest_t, hp):
    T, W = hp.shape
    tm = _tile(T, DISPATCH_TM)
    return pl.pallas_call(
        _dispatch_kernel,
        out_shape=jax.ShapeDtypeStruct((T * TOP_K, W), hp.dtype),
        grid=(T // tm,),
        in_specs=[pl.BlockSpec((TOP_K, tm), lambda i: (0, i), memory_space=pltpu.SMEM),
                  pl.BlockSpec((tm, W), lambda i: (i, 0))],
        out_specs=pl.BlockSpec(memory_space=pl.ANY),
        scratch_shapes=[pltpu.SemaphoreType.DMA(())],
        compiler_params=_cparams(("arbitrary",)),
        name="dispatch",
    )(dest_t, hp)


def _moe_kernel(ib_ref, ie_ref, lo_ref, hi_ref, n_ref, x_ref, wgu_ref, wd_ref, y_ref, acc_ref,
                wgu_b, wd_b, *, ff):
    i = pl.program_id(0)
    n = n_ref[0]
    last_i = ib_ref.shape[0] - 1
    tm = x_ref.shape[0]

    @pl.when(i < n)
    def _():
        b = ib_ref[i]
        first = jnp.logical_or(i == 0, ib_ref[jnp.maximum(i - 1, 0)] != b)
        last = jnp.logical_or(i == n - 1, ib_ref[jnp.minimum(i + 1, last_i)] != b)
        xlo, xhi = _unpack_pairs(x_ref[...])
        @pl.when(jnp.logical_or(i == 0, ie_ref[jnp.maximum(i - 1, 0)] != ie_ref[i]))
        def _():
            wgu_b[...] = wgu_ref[0].astype(BF16)
            wd_b[...] = wd_ref[0].astype(BF16)

        xb = jnp.concatenate([xlo.astype(BF16), xhi.astype(BF16)], axis=1)
        h = jnp.dot(xb, wgu_b[...], preferred_element_type=F32)
        a = (jax.nn.silu(h[:, :ff]) * h[:, ff:]).astype(BF16)
        y = jnp.dot(a, wd_b[...], preferred_element_type=F32)
        rows = lax.broadcasted_iota(jnp.int32, (tm, 1), 0)
        keep = jnp.logical_and(rows >= lo_ref[i], rows < hi_ref[i])
        y = jnp.where(keep, y, 0.0)

        @pl.when(jnp.logical_and(first, last))
        def _():
            y_ref[...] = _pack_pairs(y)

        @pl.when(jnp.logical_and(first, jnp.logical_not(last)))
        def _():
            acc_ref[...] = y

        @pl.when(jnp.logical_not(first))
        def _():
            acc_ref[...] += y

        @pl.when(jnp.logical_and(last, jnp.logical_not(first)))
        def _():
            y_ref[...] = _pack_pairs(acc_ref[...])


def _moe_experts(ib, ie, lo, hi, n_items, xs, wgu, wd, tm):
    N, W = xs.shape
    ff = wd.shape[1]
    D = wd.shape[2]
    grid_spec = pltpu.PrefetchScalarGridSpec(
        num_scalar_prefetch=5,
        grid=(ib.shape[0],),
        in_specs=[pl.BlockSpec((tm, W), lambda i, ib, ie, lo, hi, n: (ib[i], 0)),
                  pl.BlockSpec((1, D, 2 * ff), lambda i, ib, ie, lo, hi, n: (ie[i], 0, 0)),
                  pl.BlockSpec((1, ff, D), lambda i, ib, ie, lo, hi, n: (ie[i], 0, 0))],
        out_specs=pl.BlockSpec((tm, W), lambda i, ib, ie, lo, hi, n: (ib[i], 0)),
        scratch_shapes=[pltpu.VMEM((tm, D), F32), pltpu.VMEM((D, 2 * ff), BF16), pltpu.VMEM((ff, D), BF16)],
    )
    return pl.pallas_call(
        functools.partial(_moe_kernel, ff=ff),
        out_shape=jax.ShapeDtypeStruct((N, W), jnp.uint32),
        grid_spec=grid_spec,
        compiler_params=_cparams(("arbitrary",)),
        name="moe_experts",
    )(ib, ie, lo, hi, n_items, xs, wgu, wd)


def _combine_kernel(dest_ref, dnext_ref, ys_ref, w_ref, h_ref, hb_ref, wsg_ref, wsd_ref, g_ref, b_ref, o_ref,
                    ybuf, routed, sems, *, ff):
    i = pl.program_id(0)
    last = pl.num_programs(0) - 1
    tm = h_ref.shape[0]
    half = ybuf.shape[3]
    sub = 8
    slot = lax.rem(i, 2)
    nslot = 1 - slot

    def issue(idx_ref, r, to_slot):
        for k in range(TOP_K):
            pltpu.make_async_copy(ys_ref.at[pl.ds(idx_ref[k, r], 1)], ybuf.at[to_slot, k, pl.ds(r, 1)],
                                  sems.at[to_slot]).start()

    @pl.when(i == 0)
    def _():
        def first(r, c):
            issue(dest_ref, r, slot)
            return c
        lax.fori_loop(0, tm, first, 0)

    for k in range(TOP_K):
        pltpu.make_async_copy(ys_ref.at[pl.ds(0, tm)], ybuf.at[slot, k], sems.at[slot]).wait()

    def sum_rows(r0):
        w = w_ref[pl.ds(r0, sub), :]
        rlo = jnp.zeros((sub, half), F32)
        rhi = jnp.zeros((sub, half), F32)
        for k in range(TOP_K):
            lo, hi = _unpack_pairs(ybuf[slot, k, pl.ds(r0, sub), :])
            rlo = rlo + lo * w[:, k:k + 1]
            rhi = rhi + hi * w[:, k:k + 1]
        routed[pl.ds(r0, sub), :half] = rlo
        routed[pl.ds(r0, sub), half:] = rhi

    @pl.when(i < last)
    def _():
        def fused(t, c):
            r0 = pl.multiple_of(t * sub, sub)
            sum_rows(r0)
            for j in range(sub):
                issue(dnext_ref, r0 + j, nslot)
            return c
        lax.fori_loop(0, tm // sub, fused, 0)

    @pl.when(i == last)
    def _():
        def tail(t, c):
            sum_rows(pl.multiple_of(t * sub, sub))
            return c
        lax.fori_loop(0, tm // sub, tail, 0)

    s = jnp.dot(hb_ref[...], wsg_ref[...], preferred_element_type=F32)
    a = (jax.nn.silu(s[:, :ff]) * s[:, ff:]).astype(BF16)
    shared = jnp.dot(a, wsd_ref[...], preferred_element_type=F32)
    o_ref[...] = _layer_norm(ALPHA * h_ref[...] + routed[...] + shared, g_ref[...], b_ref[...])


def _moe_combine(dest_t, ys, w, h, hb, wsg, wsd, g, b):
    T, D = h.shape
    ff = wsd.shape[0]
    tm = _tile(T, COMBINE_TM)
    nt = T // tm
    row = lambda wd_: pl.BlockSpec((tm, wd_), lambda i: (i, 0))
    full = lambda a: pl.BlockSpec(a.shape, lambda i: (0,) * a.ndim)
    return pl.pallas_call(
        functools.partial(_combine_kernel, ff=ff),
        out_shape=jax.ShapeDtypeStruct((T, D), F32),
        grid=(T // tm,),
        in_specs=[pl.BlockSpec((TOP_K, tm), lambda i: (0, i), memory_space=pltpu.SMEM),
                  pl.BlockSpec((TOP_K, tm), lambda i: (0, jnp.minimum(i + 1, nt - 1)), memory_space=pltpu.SMEM),
                  pl.BlockSpec(memory_space=pl.ANY), row(TOP_K), row(D), row(D),
                  full(wsg), full(wsd), full(g), full(b)],
        out_specs=row(D),
        scratch_shapes=[pltpu.VMEM((2, TOP_K, tm, ys.shape[1]), ys.dtype), pltpu.VMEM((tm, D), F32),
                        pltpu.SemaphoreType.DMA((2,))],
        compiler_params=_cparams(("arbitrary",)),
        name="moe_combine",
    )(dest_t, dest_t, ys, w, h, hb, wsg, wsd, g, b)


def _rot_cols(w):
    half = w.shape[-1] // 2
    return jnp.concatenate([-w[..., half:], w[..., :half]], axis=-1)


def _t5_bucket(dist):
    n = jnp.maximum(dist, 0)
    max_exact = REL_BUCKETS // 2
    large = max_exact + (jnp.log(jnp.maximum(n, 1).astype(F32) / max_exact)
                         / math.log(REL_MAX_DIST / max_exact) * (REL_BUCKETS - max_exact)).astype(jnp.int32)
    large = jnp.minimum(large, REL_BUCKETS - 1)
    return jnp.where(n < max_exact, n, large)


def _layout(widths):
    order = sorted(range(len(widths)), key=lambda j: -widths[j][1])
    lay, off = {}, 0
    for j in order:
        name, w = widths[j]
        assert off % w == 0, (name, off, w)
        lay[name] = off
        off += w
    return lay, off, [widths[j][0] for j in order]


def _work_items(gs, tm, n_blocks):
    E = gs.shape[0]
    end = jnp.cumsum(gs)
    start = end - gs
    b0 = start // tm
    n_e = jnp.where(gs > 0, (end + tm - 1) // tm - b0, 0)
    icum = jnp.cumsum(n_e)
    n_items = icum[-1]
    cap = n_blocks + E
    idx = jnp.arange(cap, dtype=jnp.int32)
    ie = jnp.minimum(jnp.sum(idx[:, None] >= icum[None, :], axis=1), E - 1).astype(jnp.int32)
    ib = (b0[ie] + idx - (icum - n_e)[ie]).astype(jnp.int32)
    lo = jnp.maximum(start[ie] - ib * tm, 0)
    hi = jnp.minimum(end[ie] - ib * tm, tm)
    ok = idx < n_items
    ib = jnp.where(ok, ib, n_blocks - 1).astype(jnp.int32)
    lo = jnp.where(ok, lo, 0).astype(jnp.int32)
    hi = jnp.where(ok, hi, 0).astype(jnp.int32)
    return start, ib, ie, lo, hi, n_items.astype(jnp.int32).reshape(1)


def kernel(x, positions, w_in, b_gate, q_norm_g, kv_norm_g, w_uq, w_uk, w_uv, swa_sinks, rel_table,
           w_br_mla, w_br_swa, w_out, ln1_g, ln1_b, w_router, router_bias, w_gate_up, w_down,
           w_shared_gate_up, w_shared_down, ln2_g, ln2_b):
    B, S, D = x.shape
    T = B * S
    H = MLA_HEADS
    E = N_EXPERTS
    assert w_in.shape[0] == DEPTH == 1
    qr, kvr = w_uq.shape[1], w_uk.shape[1]
    rd = MLA_ROPE_DIM
    qw = SWA_Q_HEADS * SWA_HEAD_DIM
    kw = SWA_KV_HEADS * SWA_HEAD_DIM

    sizes = (qr, kvr, rd, qw, kw, kw, N_BRANCHES * D)
    bounds = np.cumsum(sizes)[:-1].tolist()
    w_cq, w_ckv, w_kr, w_qs, w_ks, w_vs, w_gate = jnp.split(w_in[0], bounds, axis=-1)
    zpad = jnp.zeros((D, LANE - rd), F32)
    segs = {"g0": w_gate[:, :D], "g1": w_gate[:, D:], "q_s": w_qs * (SWA_HEAD_DIM ** -0.5), "c_q": w_cq, "c_kv": w_ckv,
            "k_s": w_ks, "v_s": w_vs,
            "kra": jnp.concatenate([w_kr, zpad], axis=1),
            "krb": jnp.concatenate([_rot_cols(w_kr), zpad], axis=1)}
    lay, total, order = _layout([(n, int(a.shape[1])) for n, a in segs.items()])
    w_p = jnp.concatenate([segs[n].astype(BF16) for n in order], axis=1)

    wq = w_uq[0].reshape(qr, H, MLA_NOPE_DIM + rd)
    wq_nope, wq_rope = wq[..., :MLA_NOPE_DIM], wq[..., MLA_NOPE_DIM:]
    z64 = jnp.zeros((qr, H, LANE - rd), F32)
    wqa = jnp.concatenate([wq_nope, wq_rope, z64], axis=-1).reshape(qr, H * 2 * LANE).astype(BF16)
    wqb = jnp.concatenate([_rot_cols(wq_rope), z64], axis=-1).reshape(qr, H * LANE).astype(BF16)
    wuk = w_uk[0].astype(BF16)
    wuv = w_uv[0].astype(BF16)

    inv = ROPE_THETA ** (-jnp.arange(0, rd, 2, dtype=F32) / rd)
    ang = positions.astype(F32).reshape(T, 1) * inv
    zc = jnp.zeros((T, LANE - rd), F32)
    cosp = jnp.concatenate([jnp.cos(ang), jnp.cos(ang), zc], axis=1)
    sinp = jnp.concatenate([jnp.sin(ang), jnp.sin(ang), zc], axis=1)

    qi = jnp.arange(BLOCK)[:, None]
    kj = jnp.arange(2 * BLOCK)[None, :]
    dist = qi + BLOCK - kj
    onehot = (_t5_bucket(dist)[None] == jnp.arange(REL_BUCKETS)[:, None, None]).astype(F32)
    bias = jnp.einsum("rh,rqk->hqk", rel_table.astype(F32), onehot, precision=lax.Precision.HIGHEST)
    in_window = (dist >= 0) & (dist < WINDOW)
    bias_masked = jnp.stack([jnp.where((in_window & (kj >= BLOCK))[None], bias, NEG_BIG),
                             jnp.where(in_window[None], bias, NEG_BIG)])

    per = E // N_GROUPS
    perm = np.arange(E).reshape(N_GROUPS, per).T.reshape(-1)
    wrt = w_router[0].T[perm].astype(BF16)
    bias_col = router_bias[0].astype(F32)[perm].reshape(E, 1)
    rt = _tile(T, ROUTER_TM)
    utri = jnp.asarray(np.triu(np.ones((rt, rt), np.float32), k=1), dtype=BF16)

    x2 = x.reshape(T, D)

    proj = _in_proj(x2, w_p)
    q, k, v = _mla_proj(proj, lay, cosp, sinp, q_norm_g[0].reshape(1, qr), kv_norm_g[0].reshape(1, kvr),
                        wqa, wqb, wuk, wuv, H)
    o_m = _mla_attn(q.reshape(B, S, -1), k.reshape(B, S, -1), v.reshape(B, S, -1), H).reshape(T, -1)
    o_s = _swa_attn(proj.reshape(B, S, total), lay, swa_sinks[0].astype(F32), bias_masked).reshape(T, qw)
    merged = _merge(o_m, o_s, proj, lay, b_gate[0], w_br_mla[0].astype(BF16), w_br_swa[0].astype(BF16))
    h, hb, hp, scores_t = _out_ln(merged, x2, w_out[0].astype(BF16), ln1_g[0].reshape(1, D),
                                  ln1_b[0].reshape(1, D), wrt)

    te_t, w_t, rk_t, cnt = _router(scores_t, bias_col, utri)
    inv_perm = np.argsort(perm)
    gs = cnt[:, 0].astype(jnp.int32)[inv_perm]
    tm = MOE_TM
    N = T * TOP_K
    assert N % tm == 0
    start, ib, ie, lo, hi, n_items = _work_items(gs, tm, N // tm)
    dest_t = rk_t + jnp.sum(jnp.where(te_t[:, :, None] == jnp.arange(E, dtype=jnp.int32),
                                      start.astype(jnp.int32), 0), axis=-1)

    xs = _dispatch(dest_t, hp)
    ys = _moe_experts(ib, ie, lo, hi, n_items, xs, w_gate_up[0], w_down[0], tm)
    out = _moe_combine(dest_t, ys, w_t.T, h, hb, w_shared_gate_up[0].astype(BF16),
                       w_shared_down[0].astype(BF16), ln2_g[0].reshape(1, D), ln2_b[0].reshape(1, D))
    return out.reshape(B, S, D)
```
